```python
import jax, jax.numpy as jnp
from jax import lax
import numpy as np

D_MODEL = 1024
BATCH = 2
SEQ = 8192
DEPTH = 1

CONV_WIDTH = D_MODEL // 2
RWKV_WIDTH = D_MODEL - CONV_WIDTH
RWKV_HEAD = 64
RWKV_HEADS = RWKV_WIDTH // RWKV_HEAD
CONV_K = 31
DECAY_LORA = 64
A_LORA = 64
GATE_LORA = 160
D_FF = 2816
FFN_CONV_K = 3
N_RWKV_IN = 3 * RWKV_WIDTH + DECAY_LORA + A_LORA + GATE_LORA
N_IN = 2 * CONV_WIDTH + N_RWKV_IN
RWKV_SPLITS = (RWKV_WIDTH, 2 * RWKV_WIDTH, 3 * RWKV_WIDTH,
               3 * RWKV_WIDTH + DECAY_LORA, 3 * RWKV_WIDTH + DECAY_LORA + A_LORA)
RMS_EPS = 1e-6
LN_EPS = 1e-5
LNX_EPS = 64e-5

kernel_name = 'hymba_conformer_rwkv7_convffn_adaln'


def rms_norm(x, g):
    xf = x.astype(jnp.float32)
    y = xf * lax.rsqrt(jnp.mean(xf * xf, axis=-1, keepdims=True) + RMS_EPS)
    return (y * g.astype(jnp.float32)).astype(x.dtype)


def layer_norm(x, eps):
    xf = x.astype(jnp.float32)
    mu = jnp.mean(xf, axis=-1, keepdims=True)
    d = xf - mu
    return d * lax.rsqrt(jnp.mean(d * d, axis=-1, keepdims=True) + eps)


def causal_dwconv(x, w, b):
    K, C = w.shape
    y = lax.conv_general_dilated(x, w[:, None, :].astype(x.dtype), window_strides=(1,),
                                 padding=[(K - 1, 0)],
                                 dimension_numbers=('NWC', 'WIO', 'NWC'),
                                 feature_group_count=C)
    return y + b


def wkv7_scan(r, decay, k, v, kk, a):
    B, T, H, N = r.shape

    def step(S, inp):
        r_t, w_t, k_t, v_t, kk_t, a_t = inp
        s_kk = jnp.einsum('bhij,bhj->bhi', S, kk_t)
        S = (S * w_t[:, :, None, :] - s_kk[..., None] * (a_t * kk_t)[:, :, None, :]
             + v_t[..., None] * k_t[:, :, None, :])
        return S, jnp.einsum('bhij,bhj->bhi', S, r_t)

    xs = tuple(jnp.moveaxis(z.astype(jnp.float32), 1, 0) for z in (r, decay, k, v, kk, a))
    S0 = jnp.zeros((B, H, N, N), jnp.float32)
    _, ys = lax.scan(step, S0, xs)
    return jnp.moveaxis(ys, 0, 1)


def token_mix(h, w_in, conv_dw_w, conv_dw_b, conv_ln_g, conv_ln_b, rwkv_mu, w0, w2,
              a0, a2, g2, k_k, k_a, r_k, lnx_g, lnx_b, w_out):
    B, T, _ = h.shape
    f32 = jnp.float32
    p = h @ w_in
    p_conv, p_rw = p[..., :2 * CONV_WIDTH], p[..., 2 * CONV_WIDTH:]

    u, u_gate = jnp.split(p_conv, 2, axis=-1)
    u = u * jax.nn.sigmoid(u_gate)
    u = causal_dwconv(u, conv_dw_w, conv_dw_b)
    u = jax.nn.silu((layer_norm(u, LN_EPS) * conv_ln_g + conv_ln_b).astype(h.dtype))

    prev = jnp.pad(p_rw, ((0, 0), (1, 0), (0, 0)))[:, :-1]
    p_rw = p_rw + (prev - p_rw) * rwkv_mu
    r, k, v, wd, ad, gd = jnp.split(p_rw, RWKV_SPLITS, axis=-1)
    w_log = -jax.nn.softplus(-(w0 + jnp.tanh(wd) @ w2).astype(f32)) - 0.5
    decay = jnp.exp(-jnp.exp(w_log))
    a = jax.nn.sigmoid(a0 + ad @ a2)
    g = jax.nn.sigmoid(gd) @ g2
    heads = lambda z: z.reshape(B, T, RWKV_HEADS, RWKV_HEAD)
    kk = heads(k * k_k).astype(f32)
    kk = kk / jnp.maximum(jnp.sqrt(jnp.sum(kk * kk, axis=-1, keepdims=True)), 1e-12)
    k = k * (1.0 + (a - 1.0) * k_a)
    rh, kh, vh = heads(r), heads(k), heads(v)
    y = wkv7_scan(rh, heads(decay), kh, vh, kk, heads(a))
    y = layer_norm(y, LNX_EPS).reshape(B, T, RWKV_WIDTH) * lnx_g + lnx_b
    bonus = jnp.sum((rh * kh * r_k).astype(f32), axis=-1, keepdims=True) * vh.astype(f32)
    y = ((y + bonus.reshape(B, T, RWKV_WIDTH)) * g).astype(h.dtype)

    return jnp.concatenate([u, y], axis=-1) @ w_out


def channel_mix(h, w_up, ffn_dw_w, ffn_dw_b, w_down):
    z = causal_dwconv(h @ w_up, ffn_dw_w, ffn_dw_b)
    z_gate, z_val = jnp.split(z, 2, axis=-1)
    return (jax.nn.silu(z_gate) * z_val) @ w_down


def setup_inputs(seed: int = 0) -> dict:
    key = jax.random.key(seed)
    ks = jax.random.split(key, 32)
    L = DEPTH
    nrm = lambda k, shape, s: jax.random.normal(k, shape, jnp.float32) * s
    return {
        'x': nrm(ks[0], (BATCH, SEQ, D_MODEL), 1.0),
        'c': nrm(ks[1], (BATCH, D_MODEL), 1.0),
        'ada_w': nrm(ks[2], (L, D_MODEL, 6 * D_MODEL), 0.5 * D_MODEL ** -0.5),
        'ada_b': nrm(ks[3], (L, 6 * D_MODEL), 0.02),
        'mix_pre_g': 1.0 + nrm(ks[4], (L, D_MODEL), 0.05),
        'mix_post_g': 1.0 + nrm(ks[5], (L, D_MODEL), 0.05),
        'w_in': nrm(ks[6], (L, D_MODEL, N_IN), D_MODEL ** -0.5),
        'conv_dw_w': nrm(ks[7], (L, CONV_K, CONV_WIDTH), CONV_K ** -0.5),
        'conv_dw_b': nrm(ks[8], (L, CONV_WIDTH), 0.02),
        'conv_ln_g': 1.0 + nrm(ks[9], (L, CONV_WIDTH), 0.05),
        'conv_ln_b': nrm(ks[10], (L, CONV_WIDTH), 0.02),
        'rwkv_mu': jax.random.uniform(ks[11], (L, N_RWKV_IN), jnp.float32),
        'w0': jax.random.uniform(ks[12], (L, RWKV_WIDTH), jnp.float32, minval=-5.0, maxval=1.0),
        'w2': nrm(ks[13], (L, DECAY_LORA, RWKV_WIDTH), 0.5 * DECAY_LORA ** -0.5),
        'a0': nrm(ks[14], (L, RWKV_WIDTH), 0.5),
        'a2': nrm(ks[15], (L, A_LORA, RWKV_WIDTH), 0.5 * A_LORA ** -0.5),
        'g2': nrm(ks[16], (L, GATE_LORA, RWKV_WIDTH), GATE_LORA ** -0.5),
        'k_k': 0.85 + nrm(ks[17], (L, RWKV_WIDTH), 0.05),
        'k_a': 1.0 + nrm(ks[18], (L, RWKV_WIDTH), 0.05),
        'r_k': nrm(ks[19], (L, RWKV_HEADS, RWKV_HEAD), 0.1),
        'lnx_g': 1.0 + nrm(ks[20], (L, RWKV_WIDTH), 0.05),
        'lnx_b': nrm(ks[21], (L, RWKV_WIDTH), 0.02),
        'w_out': nrm(ks[22], (L, D_MODEL, D_MODEL), D_MODEL ** -0.5),
        'ffn_pre_g': 1.0 + nrm(ks[23], (L, D_MODEL), 0.05),
        'ffn_post_g': 1.0 + nrm(ks[24], (L, D_MODEL), 0.05),
        'w_up': nrm(ks[25], (L, D_MODEL, 2 * D_FF), D_MODEL ** -0.5),
        'ffn_dw_w': nrm(ks[26], (L, FFN_CONV_K, 2 * D_FF), FFN_CONV_K ** -0.5),
        'ffn_dw_b': nrm(ks[27], (L, 2 * D_FF), 0.02),
        'w_down': nrm(ks[28], (L, D_FF, D_MODEL), D_FF ** -0.5),
    }


def reference(x, c, ada_w, ada_b, mix_pre_g, mix_post_g, w_in, conv_dw_w, conv_dw_b,
              conv_ln_g, conv_ln_b, rwkv_mu, w0, w2, a0, a2, g2, k_k, k_a, r_k,
              lnx_g, lnx_b, w_out, ffn_pre_g, ffn_post_g, w_up, ffn_dw_w, ffn_dw_b,
              w_down):
    for l in range(DEPTH):
        mod = jax.nn.silu(c) @ ada_w[l] + ada_b[l]
        sh_m, sc_m, gt_m, sh_f, sc_f, gt_f = jnp.split(mod[:, None, :], 6, axis=-1)
        h = rms_norm(x, mix_pre_g[l]) * (1.0 + sc_m) + sh_m
        m = token_mix(h, w_in[l], conv_dw_w[l], conv_dw_b[l], conv_ln_g[l], conv_ln_b[l],
                      rwkv_mu[l], w0[l], w2[l], a0[l], a2[l], g2[l], k_k[l], k_a[l],
                      r_k[l], lnx_g[l], lnx_b[l], w_out[l])
        x = x + gt_m * rms_norm(m, mix_post_g[l])
        h = rms_norm(x, ffn_pre_g[l]) * (1.0 + sc_f) + sh_f
        f = channel_mix(h, w_up[l], ffn_dw_w[l], ffn_dw_b[l], w_down[l])
        x = x + gt_f * rms_norm(f, ffn_post_g[l])
    return x
```

```python
import functools

import jax
import jax.numpy as jnp
from jax import lax
from jax.experimental import pallas as pl
from jax.experimental.pallas import tpu as pltpu

F32 = jnp.float32
BF16 = jnp.bfloat16

HEAD = 64
CHUNK = 64
PAIR = 2 * HEAD
CONV_K = 31
FFN_K = 3
RMS_EPS = 1e-6
LN_EPS = 1e-5
LNX_EPS = 64e-5
CONV_HALO = 32
FFN_HALO = 8
FF_CHUNK = 256
VMEM_LIMIT = 56 * 1024 * 1024

NT = (((1,), (1,)), ((), ()))
TN = (((0,), (0,)), ((), ()))


def _dot(a, b):
    return jnp.dot(a, b, preferred_element_type=F32)


def _dg(a, b, dims):
    return lax.dot_general(a, b, dims, preferred_element_type=F32)


def _sigmoid(x):
    return jax.nn.sigmoid(x)


def _rms(x, g):
    ms = jnp.mean(x * x, axis=-1, keepdims=True)
    return x * lax.rsqrt(ms + RMS_EPS) * g


def _mod_kernel(c_ref, w_ref, b_ref, o_ref):
    cs = c_ref[...]
    s = cs * _sigmoid(cs)
    o_ref[...] = jnp.dot(s, w_ref[...], preferred_element_type=F32,
                         precision=lax.Precision.HIGHEST) + b_ref[...]


def _mod_call(c8, ada_w, ada_b, tn):
    d, n = ada_w.shape
    return pl.pallas_call(
        _mod_kernel,
        grid=(n // tn,),
        in_specs=[pl.BlockSpec((8, d), lambda j: (0, 0)),
                  pl.BlockSpec((d, tn), lambda j: (0, j)),
                  pl.BlockSpec((1, tn), lambda j: (0, j))],
        out_specs=pl.BlockSpec((8, tn), lambda j: (0, j)),
        out_shape=jax.ShapeDtypeStruct((8, n), F32),
        compiler_params=pltpu.CompilerParams(dimension_semantics=("arbitrary",),
                                             vmem_limit_bytes=VMEM_LIMIT),
        name="adaln_mod",
    )(c8, ada_w, ada_b)


def _inproj_kernel(x_ref, mod_ref, g_ref, w_ref, pc_ref, pr_ref):
    x = x_ref[0]
    h = _rms(x, g_ref[...]) * (1.0 + mod_ref[0, 1:2, :]) + mod_ref[0, 0:1, :]
    p = _dot(h.astype(BF16), w_ref[...])
    nc = pc_ref.shape[2]
    pc_ref[0] = p[:, :nc]
    pr_ref[0] = p[:, nc:]


def _inproj_call(x, mod, g, w, n_conv, tm):
    b, t, d = x.shape
    n = w.shape[1]
    return pl.pallas_call(
        _inproj_kernel,
        grid=(b, t // tm),
        in_specs=[pl.BlockSpec((1, tm, d), lambda bi, i: (bi, i, 0)),
                  pl.BlockSpec((1, 6, d), lambda bi, i: (bi, 0, 0)),
                  pl.BlockSpec((1, d), lambda bi, i: (0, 0)),
                  pl.BlockSpec((d, n), lambda bi, i: (0, 0))],
        out_specs=[pl.BlockSpec((1, tm, n_conv), lambda bi, i: (bi, i, 0)),
                   pl.BlockSpec((1, tm, n - n_conv), lambda bi, i: (bi, i, 0))],
        out_shape=[jax.ShapeDtypeStruct((b, t, n_conv), F32),
                   jax.ShapeDtypeStruct((b, t, n - n_conv), F32)],
        compiler_params=pltpu.CompilerParams(dimension_semantics=("arbitrary", "arbitrary"),
                                             vmem_limit_bytes=VMEM_LIMIT),
        name="inproj",
    )(x, mod, g, w)


def _conv_kernel(pc_ref, halo_ref, w_ref, b_ref, lg_ref, lb_ref, o_ref, scr, *, rb):
    i = pl.program_id(1)
    tm = pc_ref.shape[1]
    cw = o_ref.shape[2]
    cur = pc_ref[0]
    scr[CONV_HALO:CONV_HALO + tm, :] = cur[:, :cw] * _sigmoid(cur[:, cw:])
    hal = halo_ref[0]
    gh = hal[:, :cw] * _sigmoid(hal[:, cw:])
    scr[0:CONV_HALO, :] = jnp.where(i > 0, gh, 0.0)
    off = CONV_HALO - (CONV_K - 1)
    for r0 in range(0, tm, rb):
        acc = jnp.zeros((rb, cw), F32) + b_ref[...]
        for k in range(CONV_K):
            acc = acc + w_ref[k:k + 1, :] * scr[r0 + off + k:r0 + off + k + rb, :]
        mu = jnp.mean(acc, axis=-1, keepdims=True)
        dd = acc - mu
        var = jnp.mean(dd * dd, axis=-1, keepdims=True)
        z = dd * lax.rsqrt(var + LN_EPS) * lg_ref[...] + lb_ref[...]
        o_ref[0, r0:r0 + rb, :] = (z * _sigmoid(z)).astype(o_ref.dtype)


def _conv_call(pc, w, bias, lg, lb, tm, rb=32):
    b, t, n2 = pc.shape
    cw = n2 // 2
    hb = tm // CONV_HALO
    return pl.pallas_call(
        functools.partial(_conv_kernel, rb=rb),
        grid=(b, t // tm),
        in_specs=[pl.BlockSpec((1, tm, n2), lambda bi, i: (bi, i, 0)),
                  pl.BlockSpec((1, CONV_HALO, n2), lambda bi, i: (bi, jnp.maximum(i * hb - 1, 0), 0)),
                  pl.BlockSpec((CONV_K, cw), lambda bi, i: (0, 0)),
                  pl.BlockSpec((1, cw), lambda bi, i: (0, 0)),
                  pl.BlockSpec((1, cw), lambda bi, i: (0, 0)),
                  pl.BlockSpec((1, cw), lambda bi, i: (0, 0))],
        out_specs=pl.BlockSpec((1, tm, cw), lambda bi, i: (bi, i, 0)),
        out_shape=jax.ShapeDtypeStruct((b, t, cw), BF16),
        scratch_shapes=[pltpu.VMEM((tm + CONV_HALO, cw), F32)],
        compiler_params=pltpu.CompilerParams(dimension_semantics=("arbitrary", "arbitrary"),
                                             vmem_limit_bytes=VMEM_LIMIT),
        name="conv_group",
    )(pc, pc, w, bias, lg, lb)


def _rwkv_kernel(p_ref, mu_ref, w0_ref, w2_ref, a0_ref, a2_ref, g2_ref, kk_ref, ka_ref, rk_ref,
                 lg_ref, lb_ref, bd_ref, tri_ref, strict_ref, incl_ref, eye_ref,
                 o_ref,
                 s_ref, prev_ref, kt0, kt1, rt0, rt1, kh0, kh1, bh0, bh1, v0, v1,
                 ec_ref, bonus_ref, gate_ref, y_ref):
    i = pl.program_id(1)
    tb = p_ref.shape[1]
    rw = o_ref.shape[2]
    npair = rw // PAIR

    @pl.when(i == 0)
    def _():
        s_ref[...] = jnp.zeros_like(s_ref)
        prev_ref[...] = jnp.zeros_like(prev_ref)

    p = p_ref[0]
    rows = lax.broadcasted_iota(jnp.int32, (tb, 1), 0)
    prev = jnp.where(rows == 0, prev_ref[0:1, :], pltpu.roll(p, 1, axis=0))
    prev_ref[0:1, :] = p[tb - 1:tb, :]
    xs = p + (prev - p) * mu_ref[...]
    r = xs[:, 0:rw]
    k = xs[:, rw:2 * rw]
    v = xs[:, 2 * rw:3 * rw]
    o = 3 * rw
    wd = xs[:, o:o + 128]
    ad = xs[:, o + 128:o + 256]
    gd = xs[:, o + 256:o + 512]

    zw = w0_ref[...] + _dot(jnp.tanh(wd).astype(BF16), w2_ref[...])
    q = -zw
    softplus = jnp.maximum(q, 0.0) + jnp.log1p(jnp.exp(-jnp.abs(q)))
    lw = -jnp.exp(-softplus - 0.5)
    a = _sigmoid(a0_ref[...] + _dot(ad.astype(BF16), a2_ref[...]))
    gate_ref[...] = _dot(_sigmoid(gd).astype(BF16), g2_ref[...])
    bd = bd_ref[...]
    kkr = k * kk_ref[...]
    ss = _dot((kkr * kkr).astype(BF16), bd)
    kk = kkr * lax.rsqrt(jnp.maximum(ss, 1e-24))
    km = k * (1.0 + (a - 1.0) * ka_ref[...])
    bb = a * kk
    bonus_ref[...] = _dot((r * km * rk_ref[...]).astype(BF16), bd) * v

    hi = lw.astype(BF16)
    r1 = lw - hi.astype(F32)
    mid = r1.astype(BF16)
    lo = (r1 - mid.astype(F32)).astype(BF16)
    tri = tri_ref[...]
    c = _dot(tri, hi) + _dot(tri, mid) + _dot(tri, lo)
    ec = jnp.exp(c)
    ecn = jnp.exp(-c)
    ecm = jnp.exp(c - lw)
    ec_ref[...] = ec

    lane = lax.broadcasted_iota(jnp.int32, (1, rw), 1)
    even = (lane % PAIR) < HEAD

    def split(dst0, dst1, val):
        dst0[...] = jnp.where(even, val, 0.0).astype(BF16)
        dst1[...] = jnp.where(even, 0.0, val).astype(BF16)

    split(kt0, kt1, kk * ecm)
    split(rt0, rt1, r * ec)
    split(kh0, kh1, km * ecn)
    split(bh0, bh1, bb * ecn)
    split(v0, v1, v)

    strict = strict_ref[...]
    incl = incl_ref[...]
    eye = eye_ref[...]
    n_doubling = CHUNK.bit_length() - 2

    def chunk_body(ci, carry):
        r0 = pl.multiple_of(ci * CHUNK, CHUNK)
        rs = pl.ds(r0, CHUNK)
        for pi in range(npair):
            ls = slice(pi * PAIR, (pi + 1) * PAIR)
            lhs = jnp.concatenate([kt0[rs, ls], kt1[rs, ls], rt0[rs, ls], rt1[rs, ls]], axis=0)
            kb = jnp.concatenate([kh0[rs, ls], kh1[rs, ls], bh0[rs, ls], bh1[rs, ls]], axis=0)
            vs = jnp.concatenate([v0[rs, ls], v1[rs, ls]], axis=0)
            g = _dg(lhs, kb, NT)
            a_kk = (g[0:PAIR, 0:PAIR] * strict).astype(BF16)
            n_m = -(g[0:PAIR, PAIR:2 * PAIR] * strict)
            a_kr = (g[PAIR:2 * PAIR, 0:PAIR] * incl).astype(BF16)
            a_br = (g[PAIR:2 * PAIR, PAIR:2 * PAIR] * incl).astype(BF16)
            t_m = eye + n_m
            pk = n_m
            for _ in range(n_doubling):
                pkb = pk.astype(BF16)
                pk = _dot(pkb, pkb)
                t_m = t_m + _dot(t_m.astype(BF16), pk.astype(BF16))
            s = s_ref[pi]
            sb = s.astype(BF16)
            xs_ = _dg(lhs[0:PAIR], sb, NT) + _dot(a_kk, vs)
            us = _dot(t_m.astype(BF16), xs_.astype(BF16)).astype(BF16)
            ys = _dg(lhs[PAIR:2 * PAIR], sb, NT) + _dot(a_kr, vs) - _dot(a_br, us)
            y_ref[rs, ls] = ys[0:CHUNK] + ys[CHUNK:2 * CHUNK]
            ds_ = _dg(vs, kb[0:PAIR], TN) - _dg(us, kb[PAIR:2 * PAIR], TN)
            e_l = ec_ref[pl.ds(pl.multiple_of(r0 + CHUNK - 8, 8), 8), ls][7:8]
            s_ref[pi] = (s + ds_) * e_l
        return carry

    lax.fori_loop(0, tb // CHUNK, chunk_body, 0)

    y = y_ref[...]
    inv = 1.0 / HEAD
    mean = _dot(y.astype(BF16), bd) * inv
    d = y - mean
    var = _dot((d * d).astype(BF16), bd) * inv
    yl = d * lax.rsqrt(var + LNX_EPS) * lg_ref[...] + lb_ref[...]
    o_ref[0] = ((yl + bonus_ref[...]) * gate_ref[...]).astype(o_ref.dtype)


def _rwkv_call(pr, consts, tb):
    b, t, n = pr.shape
    rw = consts["w0"].shape[1]
    npair = rw // PAIR
    names = ["mu", "w0", "w2", "a0", "a2", "g2", "k_k", "k_a", "r_k", "lnx_g", "lnx_b",
             "bd", "tri", "strict", "incl", "eye"]
    ops = [consts[nm] for nm in names]
    full = lambda arr: pl.BlockSpec(arr.shape, lambda bi, i: (0,) * arr.ndim)
    blk = lambda dt: pltpu.VMEM((tb, rw), dt)
    return pl.pallas_call(
        _rwkv_kernel,
        grid=(b, t // tb),
        in_specs=[pl.BlockSpec((1, tb, n), lambda bi, i: (bi, i, 0))] + [full(x) for x in ops],
        out_specs=pl.BlockSpec((1, tb, rw), lambda bi, i: (bi, i, 0)),
        out_shape=jax.ShapeDtypeStruct((b, t, rw), BF16),
        scratch_shapes=[pltpu.VMEM((npair, PAIR, PAIR), F32), pltpu.VMEM((8, n), F32)]
                       + [blk(BF16)] * 10 + [blk(F32)] * 4,
        compiler_params=pltpu.CompilerParams(dimension_semantics=("arbitrary", "arbitrary"),
                                             vmem_limit_bytes=VMEM_LIMIT),
        name="rwkv_group",
    )(pr, *ops)


def _outproj_kernel(u_ref, y_ref, x_ref, mod_ref, g_ref, w_ref, o_ref):
    cw = u_ref.shape[2]
    m = _dot(u_ref[0], w_ref[0:cw, :]) + _dot(y_ref[0], w_ref[cw:, :])
    o_ref[0] = x_ref[0] + mod_ref[0, 2:3, :] * _rms(m, g_ref[...])


def _outproj_call(u, y, x, mod, g, w, tm):
    b, t, d = x.shape
    cw = u.shape[2]
    return pl.pallas_call(
        _outproj_kernel,
        grid=(b, t // tm),
        in_specs=[pl.BlockSpec((1, tm, cw), lambda bi, i: (bi, i, 0)),
                  pl.BlockSpec((1, tm, y.shape[2]), lambda bi, i: (bi, i, 0)),
                  pl.BlockSpec((1, tm, d), lambda bi, i: (bi, i, 0)),
                  pl.BlockSpec((1, 6, d), lambda bi, i: (bi, 0, 0)),
                  pl.BlockSpec((1, d), lambda bi, i: (0, 0)),
                  pl.BlockSpec(w.shape, lambda bi, i: (0, 0))],
        out_specs=pl.BlockSpec((1, tm, d), lambda bi, i: (bi, i, 0)),
        out_shape=jax.ShapeDtypeStruct((b, t, d), F32),
        compiler_params=pltpu.CompilerParams(dimension_semantics=("arbitrary", "arbitrary"),
                                             vmem_limit_bytes=VMEM_LIMIT),
        name="outproj",
    )(u, y, x, mod, g, w)


def _ffn_kernel(x_ref, halo_ref, mod_ref, gpre_ref, gpost_ref, wup_ref, cw_ref, cb_ref, wdn_ref,
                o_ref, z_ref, acc_ref):
    i = pl.program_id(1)
    tm = x_ref.shape[1]
    x = x_ref[0]
    xc = jnp.concatenate([halo_ref[0], x], axis=0)
    h = _rms(xc, gpre_ref[...]) * (1.0 + mod_ref[0, 4:5, :]) + mod_ref[0, 3:4, :]
    hb = h.astype(BF16)
    keep = jnp.where(i > 0, 1.0, 0.0)
    n_chunks = wdn_ref.shape[0] // FF_CHUNK
    w2c = 2 * FF_CHUNK
    for f in range(n_chunks):
        cs = slice(f * w2c, (f + 1) * w2c)
        z = _dot(hb, wup_ref[:, cs])
        z_ref[0:FFN_HALO, :] = z[0:FFN_HALO] * keep
        z_ref[FFN_HALO:, :] = z[FFN_HALO:]
        zc = cb_ref[:, cs]
        for k in range(FFN_K):
            o = FFN_HALO - (FFN_K - 1) + k
            zc = zc + cw_ref[k:k + 1, cs] * z_ref[o:o + tm, :]
        zg = zc[:, :FF_CHUNK]
        act = (zg * _sigmoid(zg) * zc[:, FF_CHUNK:]).astype(BF16)
        contrib = _dot(act, wdn_ref[f * FF_CHUNK:(f + 1) * FF_CHUNK, :])
        if f == 0:
            acc_ref[...] = contrib
        else:
            acc_ref[...] += contrib
    o_ref[0] = x + mod_ref[0, 5:6, :] * _rms(acc_ref[...], gpost_ref[...])


def _ffn_call(x1, mod, gpre, gpost, wup, cw, cb, wdn, tm):
    b, t, d = x1.shape
    hb = tm // FFN_HALO
    full = lambda arr: pl.BlockSpec(arr.shape, lambda bi, i: (0,) * arr.ndim)
    return pl.pallas_call(
        _ffn_kernel,
        grid=(b, t // tm),
        in_specs=[pl.BlockSpec((1, tm, d), lambda bi, i: (bi, i, 0)),
                  pl.BlockSpec((1, FFN_HALO, d), lambda bi, i: (bi, jnp.maximum(i * hb - 1, 0), 0)),
                  pl.BlockSpec((1, 6, d), lambda bi, i: (bi, 0, 0)),
                  full(gpre), full(gpost), full(wup), full(cw), full(cb), full(wdn)],
        out_specs=pl.BlockSpec((1, tm, d), lambda bi, i: (bi, i, 0)),
        out_shape=jax.ShapeDtypeStruct((b, t, d), F32),
        scratch_shapes=[pltpu.VMEM((tm + FFN_HALO, 2 * FF_CHUNK), F32),
                        pltpu.VMEM((tm, d), F32)],
        compiler_params=pltpu.CompilerParams(dimension_semantics=("arbitrary", "arbitrary"),
                                             vmem_limit_bytes=VMEM_LIMIT),
        name="conv_ffn",
    )(x1, x1, mod, gpre, gpost, wup, cw, cb, wdn)


def _pad_cols(w, n):
    return jnp.pad(w, ((0, 0), (0, n - w.shape[1])))


def _pad_rows(w, n):
    return jnp.pad(w, ((0, n - w.shape[0]), (0, 0)))


def _interleave_gate_val(w, d_ff):
    lead = w.shape[0]
    gate = w[:, :d_ff].reshape(lead, d_ff // FF_CHUNK, FF_CHUNK)
    val = w[:, d_ff:].reshape(lead, d_ff // FF_CHUNK, FF_CHUNK)
    return jnp.concatenate([gate, val], axis=2).reshape(lead, 2 * d_ff)


def _layer(x, mod, lw, tiles):
    b, t, d = x.shape
    cw = lw["conv_dw_w"].shape[1]
    rw = lw["w0"].shape[0]
    dl = lw["w2"].shape[0]
    al = lw["a2"].shape[0]
    gl = lw["g2"].shape[0]
    row = lambda vec: vec.reshape(1, -1)

    w_in = lw["w_in"]
    o = 2 * cw + 3 * rw
    w_in_r = jnp.concatenate(
        [w_in[:, :o], _pad_cols(w_in[:, o:o + dl], 128), _pad_cols(w_in[:, o + dl:o + dl + al], 128),
         _pad_cols(w_in[:, o + dl + al:], 256)], axis=1).astype(BF16)
    mu = row(lw["rwkv_mu"])
    o2 = 3 * rw
    mu_r = jnp.concatenate([mu[:, :o2], _pad_cols(mu[:, o2:o2 + dl], 128),
                            _pad_cols(mu[:, o2 + dl:o2 + dl + al], 128),
                            _pad_cols(mu[:, o2 + dl + al:], 256)], axis=1)

    tb = tiles["rwkv"]
    ch = jnp.arange(rw) // HEAD
    rr = jnp.arange(tb)
    pp = jnp.arange(PAIR)
    consts = {
        "mu": mu_r,
        "w0": row(lw["w0"]), "w2": _pad_rows(lw["w2"], 128).astype(BF16),
        "a0": row(lw["a0"]), "a2": _pad_rows(lw["a2"], 128).astype(BF16),
        "g2": _pad_rows(lw["g2"], 256).astype(BF16),
        "k_k": row(lw["k_k"]), "k_a": row(lw["k_a"]), "r_k": row(lw["r_k"]),
        "lnx_g": row(lw["lnx_g"]), "lnx_b": row(lw["lnx_b"]),
        "bd": (ch[:, None] == ch[None, :]).astype(BF16),
        "tri": ((rr[:, None] // CHUNK == rr[None, :] // CHUNK) & (rr[None, :] <= rr[:, None])).astype(BF16),
        "strict": ((pp[None, :] % CHUNK) < (pp[:, None] % CHUNK)).astype(F32),
        "incl": ((pp[None, :] % CHUNK) <= (pp[:, None] % CHUNK)).astype(F32),
        "eye": jnp.eye(PAIR, dtype=F32),
    }

    pc, pr = _inproj_call(x, mod, row(lw["mix_pre_g"]), w_in_r, 2 * cw, tiles["inproj"])
    u = _conv_call(pc, lw["conv_dw_w"], row(lw["conv_dw_b"]), row(lw["conv_ln_g"]),
                   row(lw["conv_ln_b"]), tiles["conv"])
    y = _rwkv_call(pr, consts, tb)
    x1 = _outproj_call(u, y, x, mod, row(lw["mix_post_g"]), lw["w_out"].astype(BF16), tiles["outproj"])

    d_ff = lw["w_down"].shape[0]
    wup = _interleave_gate_val(lw["w_up"], d_ff).astype(BF16)
    cwf = _interleave_gate_val(lw["ffn_dw_w"], d_ff)
    cbf = _interleave_gate_val(row(lw["ffn_dw_b"]), d_ff)
    return _ffn_call(x1, mod, row(lw["ffn_pre_g"]), row(lw["ffn_post_g"]), wup, cwf, cbf,
                     lw["w_down"].astype(BF16), tiles["ffn"])


def _tiles(t):
    pick = lambda pref: min(pref, t)
    return {"inproj": pick(512), "conv": pick(256), "rwkv": pick(256), "outproj": pick(512),
            "ffn": pick(256)}


def kernel(x, c, ada_w, ada_b, mix_pre_g, mix_post_g, w_in, conv_dw_w, conv_dw_b, conv_ln_g,
           conv_ln_b, rwkv_mu, w0, w2, a0, a2, g2, k_k, k_a, r_k, lnx_g, lnx_b, w_out, ffn_pre_g,
           ffn_post_g, w_up, ffn_dw_w, ffn_dw_b, w_down):
    b, t, d = x.shape
    depth = ada_w.shape[0]
    tiles = _tiles(t)
    c8 = jnp.pad(c, ((0, 8 - b), (0, 0)))
    for l in range(depth):
        mod = _mod_call(c8, ada_w[l], ada_b[l].reshape(1, -1), 1536)[:b].reshape(b, 6, d)
        lw = {"mix_pre_g": mix_pre_g[l], "mix_post_g": mix_post_g[l], "w_in": w_in[l],
              "conv_dw_w": conv_dw_w[l], "conv_dw_b": conv_dw_b[l], "conv_ln_g": conv_ln_g[l],
              "conv_ln_b": conv_ln_b[l], "rwkv_mu": rwkv_mu[l], "w0": w0[l], "w2": w2[l],
              "a0": a0[l], "a2": a2[l], "g2": g2[l], "k_k": k_k[l], "k_a": k_a[l],
              "r_k": r_k[l].reshape(-1), "lnx_g": lnx_g[l], "lnx_b": lnx_b[l], "w_out": w_out[l],
              "ffn_pre_g": ffn_pre_g[l], "ffn_post_g": ffn_post_g[l], "w_up": w_up[l],
              "ffn_dw_w": ffn_dw_w[l], "ffn_dw_b": ffn_dw_b[l], "w_down": w_down[l]}
        x = _layer(x, mod, lw, tiles)
    return x
```

```python
import functools
import math

import jax
import jax.numpy as jnp
from jax import lax
from jax.experimental import pallas as pl
from jax.experimental.pallas import tpu as pltpu

F32 = jnp.float32
BF16 = jnp.bfloat16

HEAD = 64
CHUNK = 64
PAIR = 2 * HEAD
CONV_K = 31
FFN_K = 3
RMS_EPS = 1e-6
LN_EPS = 1e-5
LNX_EPS = 64e-5
CONV_HALO = 32
FFN_HALO = 8
FF_CHUNK = 256
A_GROUP = 2
VMEM_LIMIT = 56 * 1024 * 1024

NT = (((1,), (1,)), ((), ()))
TN = (((0,), (0,)), ((), ()))


def _dot(a, b):
    return jnp.dot(a, b, preferred_element_type=F32)


def _dg(a, b, dims):
    return lax.dot_general(a, b, dims, preferred_element_type=F32)


def _sigmoid(x):
    return jax.nn.sigmoid(x)


def _rms(x, g):
    ms = jnp.mean(x * x, axis=-1, keepdims=True)
    return x * lax.rsqrt(ms + RMS_EPS) * g


def _mod_kernel(c_ref, w_ref, b_ref, o_ref):
    cs = c_ref[...]
    s = cs * _sigmoid(cs)
    o_ref[...] = jnp.dot(s, w_ref[...], preferred_element_type=F32,
                         precision=lax.Precision.HIGHEST) + b_ref[...]


def _mod_call(c8, ada_w, ada_b, tn):
    d, n = ada_w.shape
    return pl.pallas_call(
        _mod_kernel,
        grid=(n // tn,),
        in_specs=[pl.BlockSpec((8, d), lambda j: (0, 0)),
                  pl.BlockSpec((d, tn), lambda j: (0, j)),
                  pl.BlockSpec((1, tn), lambda j: (0, j))],
        out_specs=pl.BlockSpec((8, tn), lambda j: (0, j)),
        out_shape=jax.ShapeDtypeStruct((8, n), F32),
        compiler_params=pltpu.CompilerParams(dimension_semantics=("arbitrary",),
                                             vmem_limit_bytes=VMEM_LIMIT),
        name="adaln_mod",
    )(c8, ada_w, ada_b)


def _inproj_kernel(x_ref, mod_ref, g_ref, w_ref, pc_ref, pr_ref):
    x = x_ref[0]
    h = _rms(x, g_ref[...]) * (1.0 + mod_ref[0, 1:2, :]) + mod_ref[0, 0:1, :]
    p = _dot(h.astype(BF16), w_ref[...])
    nc = pc_ref.shape[2]
    pc_ref[0] = p[:, :nc]
    pr_ref[0] = p[:, nc:]


def _inproj_call(x, mod, g, w, n_conv, tm):
    b, t, d = x.shape
    n = w.shape[1]
    return pl.pallas_call(
        _inproj_kernel,
        grid=(b, t // tm),
        in_specs=[pl.BlockSpec((1, tm, d), lambda bi, i: (bi, i, 0)),
                  pl.BlockSpec((1, 6, d), lambda bi, i: (bi, 0, 0)),
                  pl.BlockSpec((1, d), lambda bi, i: (0, 0)),
                  pl.BlockSpec((d, n), lambda bi, i: (0, 0))],
        out_specs=[pl.BlockSpec((1, tm, n_conv), lambda bi, i: (bi, i, 0)),
                   pl.BlockSpec((1, tm, n - n_conv), lambda bi, i: (bi, i, 0))],
        out_shape=[jax.ShapeDtypeStruct((b, t, n_conv), F32),
                   jax.ShapeDtypeStruct((b, t, n - n_conv), F32)],
        compiler_params=pltpu.CompilerParams(dimension_semantics=("arbitrary", "arbitrary"),
                                             vmem_limit_bytes=VMEM_LIMIT),
        name="inproj",
    )(x, mod, g, w)


def _conv_kernel(pc_ref, halo_ref, w_ref, b_ref, lg_ref, lb_ref, o_ref, scr, *, rb):
    i = pl.program_id(1)
    tm = pc_ref.shape[1]
    cw = o_ref.shape[2]
    cur = pc_ref[0]
    scr[CONV_HALO:CONV_HALO + tm, :] = cur[:, :cw] * _sigmoid(cur[:, cw:])
    hal = halo_ref[0]
    gh = hal[:, :cw] * _sigmoid(hal[:, cw:])
    scr[0:CONV_HALO, :] = jnp.where(i > 0, gh, 0.0)
    off = CONV_HALO - (CONV_K - 1)
    for r0 in range(0, tm, rb):
        acc = jnp.zeros((rb, cw), F32) + b_ref[...]
        for k in range(CONV_K):
            acc = acc + w_ref[k:k + 1, :] * scr[r0 + off + k:r0 + off + k + rb, :]
        mu = jnp.mean(acc, axis=-1, keepdims=True)
        dd = acc - mu
        var = jnp.mean(dd * dd, axis=-1, keepdims=True)
        z = dd * lax.rsqrt(var + LN_EPS) * lg_ref[...] + lb_ref[...]
        o_ref[0, r0:r0 + rb, :] = (z * _sigmoid(z)).astype(o_ref.dtype)


def _conv_call(pc, w, bias, lg, lb, tm, rb=32):
    b, t, n2 = pc.shape
    cw = n2 // 2
    hb = tm // CONV_HALO
    return pl.pallas_call(
        functools.partial(_conv_kernel, rb=rb),
        grid=(b, t // tm),
        in_specs=[pl.BlockSpec((1, tm, n2), lambda bi, i: (bi, i, 0)),
                  pl.BlockSpec((1, CONV_HALO, n2), lambda bi, i: (bi, jnp.maximum(i * hb - 1, 0), 0)),
                  pl.BlockSpec((CONV_K, cw), lambda bi, i: (0, 0)),
                  pl.BlockSpec((1, cw), lambda bi, i: (0, 0)),
                  pl.BlockSpec((1, cw), lambda bi, i: (0, 0)),
                  pl.BlockSpec((1, cw), lambda bi, i: (0, 0))],
        out_specs=pl.BlockSpec((1, tm, cw), lambda bi, i: (bi, i, 0)),
        out_shape=jax.ShapeDtypeStruct((b, t, cw), BF16),
        scratch_shapes=[pltpu.VMEM((tm + CONV_HALO, cw), F32)],
        compiler_params=pltpu.CompilerParams(dimension_semantics=("arbitrary", "arbitrary"),
                                             vmem_limit_bytes=VMEM_LIMIT),
        name="conv_group",
    )(pc, pc, w, bias, lg, lb)


def _rwkv_kernel(p_ref, mu_ref, w0_ref, w2_ref, a0_ref, a2_ref, g2_ref, kk_ref, ka_ref, rk_ref,
                 lg_ref, lb_ref, bd_ref, tri_ref, strict_ref, incl_ref, eye_ref,
                 o_ref,
                 s_ref, prev_ref, kt0, kt1, rt0, rt1, kh0, kh1, bh0, bh1, v0, v1,
                 ec_ref, bonus_ref, gate_ref, y_ref, akk_ref, akr_ref, abr_ref, t_ref):
    i = pl.program_id(1)
    tb = p_ref.shape[1]
    rw = o_ref.shape[2]
    npair = rw // PAIR

    @pl.when(i == 0)
    def _():
        s_ref[...] = jnp.zeros_like(s_ref)
        prev_ref[...] = jnp.zeros_like(prev_ref)

    p = p_ref[0]
    rows = lax.broadcasted_iota(jnp.int32, (tb, 1), 0)
    prev = jnp.where(rows == 0, prev_ref[0:1, :], pltpu.roll(p, 1, axis=0))
    prev_ref[0:1, :] = p[tb - 1:tb, :]
    xs = p + (prev - p) * mu_ref[...]
    r = xs[:, 0:rw]
    k = xs[:, rw:2 * rw]
    v = xs[:, 2 * rw:3 * rw]
    o = 3 * rw
    wd = xs[:, o:o + 128]
    ad = xs[:, o + 128:o + 256]
    gd = xs[:, o + 256:o + 512]

    zw = w0_ref[...] + _dot(jnp.tanh(wd).astype(BF16), w2_ref[...])
    lw = -math.exp(-0.5) * _sigmoid(zw)
    a = _sigmoid(a0_ref[...] + _dot(ad.astype(BF16), a2_ref[...]))
    gate_ref[...] = _dot(_sigmoid(gd).astype(BF16), g2_ref[...])
    bd = bd_ref[...]
    kkr = k * kk_ref[...]
    ss = _dot((kkr * kkr).astype(BF16), bd)
    kk = kkr * lax.rsqrt(jnp.maximum(ss, 1e-24))
    km = k * (1.0 + (a - 1.0) * ka_ref[...])
    bb = a * kk
    bonus_ref[...] = _dot((r * km * rk_ref[...]).astype(BF16), bd) * v

    hi = lw.astype(BF16)
    r1 = lw - hi.astype(F32)
    mid = r1.astype(BF16)
    lo = (r1 - mid.astype(F32)).astype(BF16)
    tri = tri_ref[...]
    c = _dot(tri, hi) + _dot(tri, mid) + _dot(tri, lo)
    ec = jnp.exp(c)
    ecn = jnp.exp(-c)
    ecm = jnp.exp(c - lw)
    ec_ref[...] = ec

    lane = lax.broadcasted_iota(jnp.int32, (1, rw), 1)
    even = (lane % PAIR) < HEAD

    def split(dst0, dst1, val):
        dst0[...] = jnp.where(even, val, 0.0).astype(BF16)
        dst1[...] = jnp.where(even, 0.0, val).astype(BF16)

    split(kt0, kt1, kk * ecm)
    split(rt0, rt1, r * ec)
    split(kh0, kh1, km * ecn)
    split(bh0, bh1, bb * ecn)
    split(v0, v1, v)

    strict = strict_ref[...]
    incl = incl_ref[...]
    eye = eye_ref[...]
    n_doubling = CHUNK.bit_length() - 2

    pairs = range(npair)
    lanes = [slice(pi * PAIR, (pi + 1) * PAIR) for pi in pairs]

    def stacked(refs, rs, ls):
        return jnp.concatenate([ref[rs, ls] for ref in refs], axis=0)

    def phase_a(gi, carry):
        inst = [(gi * A_GROUP + j, pi) for j in range(A_GROUP) for pi in pairs]
        rsl = [pl.ds(pl.multiple_of(ci * CHUNK, CHUNK), CHUNK) for ci, _ in inst]
        lhs = [stacked((kt0, kt1, rt0, rt1), rs, lanes[pi]) for rs, (_, pi) in zip(rsl, inst)]
        kb = [stacked((kh0, kh1, bh0, bh1), rs, lanes[pi]) for rs, (_, pi) in zip(rsl, inst)]
        g = [_dg(l_, k_, NT) for l_, k_ in zip(lhs, kb)]
        for g_, (ci, pi) in zip(g, inst):
            akk_ref[ci, pi] = (g_[0:PAIR, 0:PAIR] * strict).astype(BF16)
            akr_ref[ci, pi] = (g_[PAIR:2 * PAIR, 0:PAIR] * incl).astype(BF16)
            abr_ref[ci, pi] = (g_[PAIR:2 * PAIR, PAIR:2 * PAIR] * incl).astype(BF16)
        pk = [-(g_[0:PAIR, PAIR:2 * PAIR] * strict) for g_ in g]
        t_m = [eye + n_ for n_ in pk]
        for _ in range(n_doubling):
            pkb = [x.astype(BF16) for x in pk]
            pk = [_dot(x, x) for x in pkb]
            t_m = [t_ + _dot(t_.astype(BF16), p_.astype(BF16)) for t_, p_ in zip(t_m, pk)]
        for t_, (ci, pi) in zip(t_m, inst):
            t_ref[ci, pi] = t_.astype(BF16)
        return carry

    lax.fori_loop(0, tb // (CHUNK * A_GROUP), phase_a, 0)

    def phase_b(ci, carry):
        r0 = pl.multiple_of(ci * CHUNK, CHUNK)
        rs = pl.ds(r0, CHUNK)
        lhs = [stacked((kt0, kt1, rt0, rt1), rs, ls) for ls in lanes]
        kb = [stacked((kh0, kh1, bh0, bh1), rs, ls) for ls in lanes]
        vs = [stacked((v0, v1), rs, ls) for ls in lanes]
        s = [s_ref[pi] for pi in pairs]
        sb = [x.astype(BF16) for x in s]
        xs_ = [_dg(lhs[pi][0:PAIR], sb[pi], NT) + _dot(akk_ref[ci, pi], vs[pi]) for pi in pairs]
        us = [_dot(t_ref[ci, pi], xs_[pi].astype(BF16)).astype(BF16) for pi in pairs]
        ds_ = [_dg(vs[pi], kb[pi][0:PAIR], TN) - _dg(us[pi], kb[pi][PAIR:2 * PAIR], TN) for pi in pairs]
        e_l = ec_ref[pl.ds(pl.multiple_of(r0 + CHUNK - 8, 8), 8), :][7:8]
        for pi in pairs:
            s_ref[pi] = (s[pi] + ds_[pi]) * e_l[:, lanes[pi]]
        ys = [_dg(lhs[pi][PAIR:2 * PAIR], sb[pi], NT) + _dot(akr_ref[ci, pi], vs[pi])
              - _dot(abr_ref[ci, pi], us[pi]) for pi in pairs]
        for pi in pairs:
            y_ref[rs, lanes[pi]] = ys[pi][0:CHUNK] + ys[pi][CHUNK:2 * CHUNK]
        return carry

    lax.fori_loop(0, tb // CHUNK, phase_b, 0)

    y = y_ref[...]
    inv = 1.0 / HEAD
    mean = _dot(y.astype(BF16), bd) * inv
    d = y - mean
    var = _dot((d * d).astype(BF16), bd) * inv
    yl = d * lax.rsqrt(var + LNX_EPS) * lg_ref[...] + lb_ref[...]
    o_ref[0] = ((yl + bonus_ref[...]) * gate_ref[...]).astype(o_ref.dtype)


def _rwkv_call(pr, consts, tb):
    b, t, n = pr.shape
    rw = consts["w0"].shape[1]
    npair = rw // PAIR
    names = ["mu", "w0", "w2", "a0", "a2", "g2", "k_k", "k_a", "r_k", "lnx_g", "lnx_b",
             "bd", "tri", "strict", "incl", "eye"]
    ops = [consts[nm] for nm in names]
    full = lambda arr: pl.BlockSpec(arr.shape, lambda bi, i: (0,) * arr.ndim)
    blk = lambda dt: pltpu.VMEM((tb, rw), dt)
    return pl.pallas_call(
        _rwkv_kernel,
        grid=(b, t // tb),
        in_specs=[pl.BlockSpec((1, tb, n), lambda bi, i: (bi, i, 0))] + [full(x) for x in ops],
        out_specs=pl.BlockSpec((1, tb, rw), lambda bi, i: (bi, i, 0)),
        out_shape=jax.ShapeDtypeStruct((b, t, rw), BF16),
        scratch_shapes=[pltpu.VMEM((npair, PAIR, PAIR), F32), pltpu.VMEM((8, n), F32)]
                       + [blk(BF16)] * 10 + [blk(F32)] * 4
                       + [pltpu.VMEM((tb // CHUNK, npair, PAIR, PAIR), BF16)] * 4,
        compiler_params=pltpu.CompilerParams(dimension_semantics=("arbitrary", "arbitrary"),
                                             vmem_limit_bytes=VMEM_LIMIT),
        name="rwkv_group",
    )(pr, *ops)


def _outproj_kernel(u_ref, y_ref, x_ref, mod_ref, g_ref, w_ref, o_ref):
    cw = u_ref.shape[2]
    m = _dot(u_ref[0], w_ref[0:cw, :]) + _dot(y_ref[0], w_ref[cw:, :])
    o_ref[0] = x_ref[0] + mod_ref[0, 2:3, :] * _rms(m, g_ref[...])


def _outproj_call(u, y, x, mod, g, w, tm):
    b, t, d = x.shape
    cw = u.shape[2]
    return pl.pallas_call(
        _outproj_kernel,
        grid=(b, t // tm),
        in_specs=[pl.BlockSpec((1, tm, cw), lambda bi, i: (bi, i, 0)),
                  pl.BlockSpec((1, tm, y.shape[2]), lambda bi, i: (bi, i, 0)),
                  pl.BlockSpec((1, tm, d), lambda bi, i: (bi, i, 0)),
                  pl.BlockSpec((1, 6, d), lambda bi, i: (bi, 0, 0)),
                  pl.BlockSpec((1, d), lambda bi, i: (0, 0)),
                  pl.BlockSpec(w.shape, lambda bi, i: (0, 0))],
        out_specs=pl.BlockSpec((1, tm, d), lambda bi, i: (bi, i, 0)),
        out_shape=jax.ShapeDtypeStruct((b, t, d), F32),
        compiler_params=pltpu.CompilerParams(dimension_semantics=("arbitrary", "arbitrary"),
                                             vmem_limit_bytes=VMEM_LIMIT),
        name="outproj",
    )(u, y, x, mod, g, w)


def _ffn_kernel(x_ref, halo_ref, mod_ref, gpre_ref, gpost_ref, wup_ref, cw_ref, cb_ref, wdn_ref,
                o_ref, z_ref, acc_ref):
    i = pl.program_id(1)
    tm = x_ref.shape[1]
    x = x_ref[0]
    xc = jnp.concatenate([halo_ref[0], x], axis=0)
    h = _rms(xc, gpre_ref[...]) * (1.0 + mod_ref[0, 4:5, :]) + mod_ref[0, 3:4, :]
    hb = h.astype(BF16)
    keep = jnp.where(i > 0, 1.0, 0.0)
    n_chunks = wdn_ref.shape[0] // FF_CHUNK
    w2c = 2 * FF_CHUNK
    up = lambda f: _dot(hb, wup_ref[:, f * w2c:(f + 1) * w2c])

    def conv_act(f, z):
        cs = slice(f * w2c, (f + 1) * w2c)
        zb = z_ref.at[f % 2]
        zb[0:FFN_HALO, :] = z[0:FFN_HALO] * keep
        zb[FFN_HALO:, :] = z[FFN_HALO:]
        zc = cb_ref[:, cs]
        for k in range(FFN_K):
            o = FFN_HALO - (FFN_K - 1) + k
            zc = zc + cw_ref[k:k + 1, cs] * zb[o:o + tm, :]
        zg = zc[:, :FF_CHUNK]
        return (zg * _sigmoid(zg) * zc[:, FF_CHUNK:]).astype(BF16)

    act = conv_act(0, up(0))
    z_next = up(1)
    for f in range(n_chunks):
        z_cur = z_next
        if f + 2 < n_chunks:
            z_next = up(f + 2)
        contrib = _dot(act, wdn_ref[f * FF_CHUNK:(f + 1) * FF_CHUNK, :])
        if f == 0:
            acc_ref[...] = contrib
        else:
            acc_ref[...] += contrib
        if f + 1 < n_chunks:
            act = conv_act(f + 1, z_cur)
    o_ref[0] = x + mod_ref[0, 5:6, :] * _rms(acc_ref[...], gpost_ref[...])


def _ffn_call(x1, mod, gpre, gpost, wup, cw, cb, wdn, tm):
    b, t, d = x1.shape
    hb = tm // FFN_HALO
    full = lambda arr: pl.BlockSpec(arr.shape, lambda bi, i: (0,) * arr.ndim)
    return pl.pallas_call(
        _ffn_kernel,
        grid=(b, t // tm),
        in_specs=[pl.BlockSpec((1, tm, d), lambda bi, i: (bi, i, 0)),
                  pl.BlockSpec((1, FFN_HALO, d), lambda bi, i: (bi, jnp.maximum(i * hb - 1, 0), 0)),
                  pl.BlockSpec((1, 6, d), lambda bi, i: (bi, 0, 0)),
                  full(gpre), full(gpost), full(wup), full(cw), full(cb), full(wdn)],
        out_specs=pl.BlockSpec((1, tm, d), lambda bi, i: (bi, i, 0)),
        out_shape=jax.ShapeDtypeStruct((b, t, d), F32),
        scratch_shapes=[pltpu.VMEM((2, tm + FFN_HALO, 2 * FF_CHUNK), F32),
                        pltpu.VMEM((tm, d), F32)],
        compiler_params=pltpu.CompilerParams(dimension_semantics=("arbitrary", "arbitrary"),
                                             vmem_limit_bytes=VMEM_LIMIT),
        name="conv_ffn",
    )(x1, x1, mod, gpre, gpost, wup, cw, cb, wdn)


def _pad_cols(w, n):
    return jnp.pad(w, ((0, 0), (0, n - w.shape[1])))


def _pad_rows(w, n):
    return jnp.pad(w, ((0, n - w.shape[0]), (0, 0)))


def _interleave_gate_val(w, d_ff):
    lead = w.shape[0]
    gate = w[:, :d_ff].reshape(lead, d_ff // FF_CHUNK, FF_CHUNK)
    val = w[:, d_ff:].reshape(lead, d_ff // FF_CHUNK, FF_CHUNK)
    return jnp.concatenate([gate, val], axis=2).reshape(lead, 2 * d_ff)


def _layer(x, mod, lw, tiles):
    b, t, d = x.shape
    cw = lw["conv_dw_w"].shape[1]
    rw = lw["w0"].shape[0]
    dl = lw["w2"].shape[0]
    al = lw["a2"].shape[0]
    gl = lw["g2"].shape[0]
    row = lambda vec: vec.reshape(1, -1)

    w_in = lw["w_in"]
    o = 2 * cw + 3 * rw
    w_in_r = jnp.concatenate(
        [w_in[:, :o], _pad_cols(w_in[:, o:o + dl], 128), _pad_cols(w_in[:, o + dl:o + dl + al], 128),
         _pad_cols(w_in[:, o + dl + al:], 256)], axis=1).astype(BF16)
    mu = row(lw["rwkv_mu"])
    o2 = 3 * rw
    mu_r = jnp.concatenate([mu[:, :o2], _pad_cols(mu[:, o2:o2 + dl], 128),
                            _pad_cols(mu[:, o2 + dl:o2 + dl + al], 128),
                            _pad_cols(mu[:, o2 + dl + al:], 256)], axis=1)

    tb = tiles["rwkv"]
    ch = jnp.arange(rw) // HEAD
    rr = jnp.arange(tb)
    pp = jnp.arange(PAIR)
    consts = {
        "mu": mu_r,
        "w0": row(lw["w0"]), "w2": _pad_rows(lw["w2"], 128).astype(BF16),
        "a0": row(lw["a0"]), "a2": _pad_rows(lw["a2"], 128).astype(BF16),
        "g2": _pad_rows(lw["g2"], 256).astype(BF16),
        "k_k": row(lw["k_k"]), "k_a": row(lw["k_a"]), "r_k": row(lw["r_k"]),
        "lnx_g": row(lw["lnx_g"]), "lnx_b": row(lw["lnx_b"]),
        "bd": (ch[:, None] == ch[None, :]).astype(BF16),
        "tri": ((rr[:, None] // CHUNK == rr[None, :] // CHUNK) & (rr[None, :] <= rr[:, None])).astype(BF16),
        "strict": ((pp[None, :] % CHUNK) < (pp[:, None] % CHUNK)).astype(F32),
        "incl": ((pp[None, :] % CHUNK) <= (pp[:, None] % CHUNK)).astype(F32),
        "eye": jnp.eye(PAIR, dtype=F32),
    }

    pc, pr = _inproj_call(x, mod, row(lw["mix_pre_g"]), w_in_r, 2 * cw, tiles["inproj"])
    u = _conv_call(pc, lw["conv_dw_w"], row(lw["conv_dw_b"]), row(lw["conv_ln_g"]),
                   row(lw["conv_ln_b"]), tiles["conv"])
    y = _rwkv_call(pr, consts, tb)
    x1 = _outproj_call(u, y, x, mod, row(lw["mix_post_g"]), lw["w_out"].astype(BF16), tiles["outproj"])

    d_ff = lw["w_down"].shape[0]
    wup = _interleave_gate_val(lw["w_up"], d_ff).astype(BF16)
    cwf = _interleave_gate_val(lw["ffn_dw_w"], d_ff)
    cbf = _interleave_gate_val(row(lw["ffn_dw_b"]), d_ff)
    return _ffn_call(x1, mod, row(lw["ffn_pre_g"]), row(lw["ffn_post_g"]), wup, cwf, cbf,
                     lw["w_down"].astype(BF16), tiles["ffn"])


def _tiles(t):
    pick = lambda pref: min(pref, t)
    return {"inproj": pick(512), "conv": pick(256), "rwkv": pick(256), "outproj": pick(512),
            "ffn": pick(256)}


def kernel(x, c, ada_w, ada_b, mix_pre_g, mix_post_g, w_in, conv_dw_w, conv_dw_b, conv_ln_g,
           conv_ln_b, rwkv_mu, w0, w2, a0, a2, g2, k_k, k_a, r_k, lnx_g, lnx_b, w_out, ffn_pre_g,
           ffn_post_g, w_up, ffn_dw_w, ffn_dw_b, w_down):
    b, t, d = x.shape
    depth = ada_w.shape[0]
    tiles = _tiles(t)
    c8 = jnp.pad(c, ((0, 8 - b), (0, 0)))
    for l in range(depth):
        mod = _mod_call(c8, ada_w[l], ada_b[l].reshape(1, -1), 1536)[:b].reshape(b, 6, d)
        lw = {"mix_pre_g": mix_pre_g[l], "mix_post_g": mix_post_g[l], "w_in": w_in[l],
              "conv_dw_w": conv_dw_w[l], "conv_dw_b": conv_dw_b[l], "conv_ln_g": conv_ln_g[l],
              "conv_ln_b": conv_ln_b[l], "rwkv_mu": rwkv_mu[l], "w0": w0[l], "w2": w2[l],
              "a0": a0[l], "a2": a2[l], "g2": g2[l], "k_k": k_k[l], "k_a": k_a[l],
              "r_k": r_k[l].reshape(-1), "lnx_g": lnx_g[l], "lnx_b": lnx_b[l], "w_out": w_out[l],
              "ffn_pre_g": ffn_pre_g[l], "ffn_post_g": ffn_post_g[l], "w_up": w_up[l],
              "ffn_dw_w": ffn_dw_w[l], "ffn_dw_b": ffn_dw_b[l], "w_down": w_down[l]}
        x = _layer(x, mod, lw, tiles)
    return x
```

```python
import functools
import math

import jax
import jax.numpy as jnp
from jax import lax
from jax.experimental import pallas as pl
from jax.experimental.pallas import tpu as pltpu

F32 = jnp.float32
BF16 = jnp.bfloat16

HEAD = 64
CHUNK = 64
PAIR = 2 * HEAD
CONV_K = 31
FFN_K = 3
RMS_EPS = 1e-6
LN_EPS = 1e-5
LNX_EPS = 64e-5
SUBLANES = 8
CONV_HALO = 32
FFN_HALO = 16
FF_CHUNK = 256
A_GROUP = 4
VMEM_LIMIT = 56 * 1024 * 1024

NT = (((1,), (1,)), ((), ()))
TN = (((0,), (0,)), ((), ()))


def _dot(a, b):
    return jnp.dot(a, b, preferred_element_type=F32)


def _dg(a, b, dims):
    return lax.dot_general(a, b, dims, preferred_element_type=F32)


def _sigmoid(x):
    return jax.nn.sigmoid(x)


def _rms(x, g):
    ms = jnp.mean(x * x, axis=-1, keepdims=True)
    return x * lax.rsqrt(ms + RMS_EPS) * g


def _mod_kernel(c_ref, w_ref, b_ref, o_ref):
    cs = c_ref[...]
    s = cs * _sigmoid(cs)
    o_ref[...] = jnp.dot(s, w_ref[...], preferred_element_type=F32,
                         precision=lax.Precision.HIGHEST) + b_ref[...]


def _mod_call(c8, ada_w, ada_b, tn):
    d, n = ada_w.shape
    return pl.pallas_call(
        _mod_kernel,
        grid=(n // tn,),
        in_specs=[pl.BlockSpec((8, d), lambda j: (0, 0)),
                  pl.BlockSpec((d, tn), lambda j: (0, j)),
                  pl.BlockSpec((1, tn), lambda j: (0, j))],
        out_specs=pl.BlockSpec((8, tn), lambda j: (0, j)),
        out_shape=jax.ShapeDtypeStruct((8, n), F32),
        compiler_params=pltpu.CompilerParams(dimension_semantics=("arbitrary",),
                                             vmem_limit_bytes=VMEM_LIMIT),
        name="adaln_mod",
    )(c8, ada_w, ada_b)


def _inproj_kernel(x_ref, mod_ref, g_ref, w_ref, pc_ref, pr_ref):
    x = x_ref[0]
    h = _rms(x, g_ref[...]) * (1.0 + mod_ref[0, 1:2, :]) + mod_ref[0, 0:1, :]
    p = _dot(h.astype(BF16), w_ref[...])
    nc = pc_ref.shape[2]
    pc_ref[0] = p[:, :nc]
    pr_ref[0] = p[:, nc:]


def _inproj_call(x, mod, g, w, n_conv, tm):
    b, t, d = x.shape
    n = w.shape[1]
    return pl.pallas_call(
        _inproj_kernel,
        grid=(b, t // tm),
        in_specs=[pl.BlockSpec((1, tm, d), lambda bi, i: (bi, i, 0)),
                  pl.BlockSpec((1, 6, d), lambda bi, i: (bi, 0, 0)),
                  pl.BlockSpec((1, d), lambda bi, i: (0, 0)),
                  pl.BlockSpec((d, n), lambda bi, i: (0, 0))],
        out_specs=[pl.BlockSpec((1, tm, n_conv), lambda bi, i: (bi, i, 0)),
                   pl.BlockSpec((1, tm, n - n_conv), lambda bi, i: (bi, i, 0))],
        out_shape=[jax.ShapeDtypeStruct((b, t, n_conv), F32),
                   jax.ShapeDtypeStruct((b, t, n - n_conv), F32)],
        compiler_params=pltpu.CompilerParams(dimension_semantics=("arbitrary", "arbitrary"),
                                             vmem_limit_bytes=VMEM_LIMIT),
        name="inproj",
    )(x, mod, g, w)


def _conv_kernel(pc_ref, halo_ref, w_ref, b_ref, lg_ref, lb_ref, o_ref, scr, shifted, *, rb):
    i = pl.program_id(1)
    tm = pc_ref.shape[1]
    cw = o_ref.shape[2]
    cur = pc_ref[0]
    scr[CONV_HALO:CONV_HALO + tm, :] = cur[:, :cw] * _sigmoid(cur[:, cw:])
    hal = halo_ref[0]
    gh = hal[:, :cw] * _sigmoid(hal[:, cw:])
    scr[0:CONV_HALO, :] = jnp.where(i > 0, gh, 0.0)
    span = tm + CONV_HALO - SUBLANES
    for r in range(1, SUBLANES):
        shifted[r - 1, 0:span, :] = scr[r:r + span, :]
    off = CONV_HALO - (CONV_K - 1)
    for r0 in range(0, tm, rb):
        acc = jnp.zeros((rb // SUBLANES, SUBLANES, cw), F32) + b_ref[...]
        for k in range(CONV_K):
            r = (off + k) % SUBLANES
            a0 = r0 + off + k - r
            src = scr[a0:a0 + rb, :] if r == 0 else shifted[r - 1, a0:a0 + rb, :]
            acc = acc + w_ref[k] * src.reshape(rb // SUBLANES, SUBLANES, cw)
        acc = acc.reshape(rb, cw)
        mu = jnp.mean(acc, axis=-1, keepdims=True)
        dd = acc - mu
        var = jnp.mean(dd * dd, axis=-1, keepdims=True)
        z = dd * lax.rsqrt(var + LN_EPS) * lg_ref[...] + lb_ref[...]
        o_ref[0, r0:r0 + rb, :] = (z * _sigmoid(z)).astype(o_ref.dtype)


def _conv_call(pc, w, bias, lg, lb, tm, rb=32):
    b, t, n2 = pc.shape
    cw = n2 // 2
    hb = tm // CONV_HALO
    return pl.pallas_call(
        functools.partial(_conv_kernel, rb=rb),
        grid=(b, t // tm),
        in_specs=[pl.BlockSpec((1, tm, n2), lambda bi, i: (bi, i, 0)),
                  pl.BlockSpec((1, CONV_HALO, n2), lambda bi, i: (bi, jnp.maximum(i * hb - 1, 0), 0)),
                  pl.BlockSpec((CONV_K, SUBLANES, cw), lambda bi, i: (0, 0, 0)),
                  pl.BlockSpec((1, cw), lambda bi, i: (0, 0)),
                  pl.BlockSpec((1, cw), lambda bi, i: (0, 0)),
                  pl.BlockSpec((1, cw), lambda bi, i: (0, 0))],
        out_specs=pl.BlockSpec((1, tm, cw), lambda bi, i: (bi, i, 0)),
        out_shape=jax.ShapeDtypeStruct((b, t, cw), BF16),
        scratch_shapes=[pltpu.VMEM((tm + CONV_HALO, cw), F32),
                        pltpu.VMEM((SUBLANES - 1, tm + CONV_HALO - SUBLANES, cw), F32)],
        compiler_params=pltpu.CompilerParams(dimension_semantics=("arbitrary", "arbitrary"),
                                             vmem_limit_bytes=VMEM_LIMIT),
        name="conv_group",
    )(pc, pc, w, bias, lg, lb)


def _rwkv_kernel(p_ref, mu_ref, w0_ref, w2_ref, a0_ref, a2_ref, g2_ref, kk_ref, ka_ref, rk_ref,
                 lg_ref, lb_ref, bd_ref, tri_ref, strict_ref, incl_ref, eye_ref,
                 o_ref,
                 s_ref, prev_ref, kt0, kt1, rt0, rt1, kh0, kh1, bh0, bh1, v0, v1,
                 ec_ref, bonus_ref, gate_ref, y_ref, akk_ref, akr_ref, abr_ref, t_ref):
    i = pl.program_id(0)
    nb, tb, n_in = p_ref.shape
    rw = o_ref.shape[2]
    npair = rw // PAIR
    cpb = tb // CHUNK

    @pl.when(i == 0)
    def _():
        s_ref[...] = jnp.zeros_like(s_ref)
        prev_ref[...] = jnp.zeros_like(prev_ref)

    p = p_ref[...].reshape(nb * tb, n_in)
    rows = lax.broadcasted_iota(jnp.int32, (nb * tb, 1), 0)
    prev = pltpu.roll(p, 1, axis=0)
    for bi in range(nb):
        prev = jnp.where(rows == bi * tb, prev_ref[bi:bi + 1, :], prev)
    for bi in range(nb):
        prev_ref[bi:bi + 1, :] = p[(bi + 1) * tb - 1:(bi + 1) * tb, :]
    xs = p + (prev - p) * mu_ref[...]
    r = xs[:, 0:rw]
    k = xs[:, rw:2 * rw]
    v = xs[:, 2 * rw:3 * rw]
    o = 3 * rw
    wd = xs[:, o:o + 128]
    ad = xs[:, o + 128:o + 256]
    gd = xs[:, o + 256:o + 512]

    zw = w0_ref[...] + _dot(jnp.tanh(wd).astype(BF16), w2_ref[...])
    lw = -math.exp(-0.5) * _sigmoid(zw)
    a = _sigmoid(a0_ref[...] + _dot(ad.astype(BF16), a2_ref[...]))
    gate_ref[...] = _dot(_sigmoid(gd).astype(BF16), g2_ref[...])
    bd = bd_ref[...]
    kkr = k * kk_ref[...]
    ss = _dot((kkr * kkr).astype(BF16), bd)
    kk = kkr * lax.rsqrt(jnp.maximum(ss, 1e-24))
    km = k * (1.0 + (a - 1.0) * ka_ref[...])
    bb = a * kk
    bonus_ref[...] = _dot((r * km * rk_ref[...]).astype(BF16), bd) * v

    hi = lw.astype(BF16)
    r1 = lw - hi.astype(F32)
    mid = r1.astype(BF16)
    lo = (r1 - mid.astype(F32)).astype(BF16)
    tri = tri_ref[...]
    seq = lambda z, bi: z[bi * tb:(bi + 1) * tb]
    c = jnp.concatenate([_dot(tri, seq(hi, bi)) + _dot(tri, seq(mid, bi)) + _dot(tri, seq(lo, bi))
                         for bi in range(nb)], axis=0)
    ec = jnp.exp(c)
    ecn = jnp.exp(-c)
    ecm = jnp.exp(c - lw)
    ec_ref[...] = ec

    lane = lax.broadcasted_iota(jnp.int32, (1, rw), 1)
    even = (lane % PAIR) < HEAD

    def split(dst0, dst1, val):
        dst0[...] = jnp.where(even, val, 0.0).astype(BF16)
        dst1[...] = jnp.where(even, 0.0, val).astype(BF16)

    split(kt0, kt1, kk * ecm)
    split(rt0, rt1, r * ec)
    split(kh0, kh1, km * ecn)
    split(bh0, bh1, bb * ecn)
    split(v0, v1, v)

    strict = strict_ref[...]
    incl = incl_ref[...]
    eye = eye_ref[...]
    n_doubling = CHUNK.bit_length() - 2

    pairs = range(npair)
    lanes = [slice(pi * PAIR, (pi + 1) * PAIR) for pi in pairs]

    def stacked(refs, rs, ls):
        return jnp.concatenate([ref[rs, ls] for ref in refs], axis=0)

    def phase_a(gi, carry):
        inst = [(gi * A_GROUP + j, pi) for j in range(A_GROUP) for pi in pairs]
        rsl = [pl.ds(pl.multiple_of(ci * CHUNK, CHUNK), CHUNK) for ci, _ in inst]
        lhs = [stacked((kt0, kt1, rt0, rt1), rs, lanes[pi]) for rs, (_, pi) in zip(rsl, inst)]
        kb = [stacked((kh0, kh1, bh0, bh1), rs, lanes[pi]) for rs, (_, pi) in zip(rsl, inst)]
        g = [_dg(l_, k_, NT) for l_, k_ in zip(lhs, kb)]
        for g_, (ci, pi) in zip(g, inst):
            akk_ref[ci, pi] = (g_[0:PAIR, 0:PAIR] * strict).astype(BF16)
            akr_ref[ci, pi] = (g_[PAIR:2 * PAIR, 0:PAIR] * incl).astype(BF16)
            abr_ref[ci, pi] = (g_[PAIR:2 * PAIR, PAIR:2 * PAIR] * incl).astype(BF16)
        pk = [-(g_[0:PAIR, PAIR:2 * PAIR] * strict) for g_ in g]
        t_m = [eye + n_ for n_ in pk]
        for _ in range(n_doubling):
            pkb = [x.astype(BF16) for x in pk]
            pk = [_dot(x, x) for x in pkb]
            t_m = [t_ + _dot(t_.astype(BF16), p_.astype(BF16)) for t_, p_ in zip(t_m, pk)]
        for t_, (ci, pi) in zip(t_m, inst):
            t_ref[ci, pi] = t_.astype(BF16)
        return carry

    lax.fori_loop(0, nb * cpb // A_GROUP, phase_a, 0)

    def phase_b(j, carry):
        chains = [(bi, pi) for bi in range(nb) for pi in pairs]
        cis = [bi * cpb + j for bi, _ in chains]
        rsl = [pl.ds(pl.multiple_of(ci * CHUNK, CHUNK), CHUNK) for ci in cis]
        lhs = [stacked((kt0, kt1, rt0, rt1), rs, lanes[pi]) for rs, (_, pi) in zip(rsl, chains)]
        kb = [stacked((kh0, kh1, bh0, bh1), rs, lanes[pi]) for rs, (_, pi) in zip(rsl, chains)]
        vs = [stacked((v0, v1), rs, lanes[pi]) for rs, (_, pi) in zip(rsl, chains)]
        s = [s_ref[bi * npair + pi] for bi, pi in chains]
        sb = [x.astype(BF16) for x in s]
        xs_ = [_dg(l_[0:PAIR], sb_, NT) + _dot(akk_ref[ci, pi], v_)
               for l_, sb_, v_, ci, (_, pi) in zip(lhs, sb, vs, cis, chains)]
        us = [_dot(t_ref[ci, pi], x_.astype(BF16)).astype(BF16) for x_, ci, (_, pi) in zip(xs_, cis, chains)]
        ds_ = [_dg(v_, k_[0:PAIR], TN) - _dg(u_, k_[PAIR:2 * PAIR], TN) for v_, k_, u_ in zip(vs, kb, us)]
        for n_, (s_, d_, ci, (bi, pi)) in enumerate(zip(s, ds_, cis, chains)):
            e_l = ec_ref[pl.ds(pl.multiple_of(ci * CHUNK + CHUNK - 8, 8), 8), lanes[pi]][7:8]
            s_ref[bi * npair + pi] = (s_ + d_) * e_l
        ys = [_dg(l_[PAIR:2 * PAIR], sb_, NT) + _dot(akr_ref[ci, pi], v_) - _dot(abr_ref[ci, pi], u_)
              for l_, sb_, v_, u_, ci, (_, pi) in zip(lhs, sb, vs, us, cis, chains)]
        for y_, rs, (_, pi) in zip(ys, rsl, chains):
            y_ref[rs, lanes[pi]] = y_[0:CHUNK] + y_[CHUNK:2 * CHUNK]
        return carry

    lax.fori_loop(0, cpb, phase_b, 0)

    y = y_ref[...]
    inv = 1.0 / HEAD
    mean = _dot(y.astype(BF16), bd) * inv
    d = y - mean
    var = _dot((d * d).astype(BF16), bd) * inv
    yl = d * lax.rsqrt(var + LNX_EPS) * lg_ref[...] + lb_ref[...]
    o_ref[...] = ((yl + bonus_ref[...]) * gate_ref[...]).astype(o_ref.dtype).reshape(nb, tb, rw)


def _rwkv_call(pr, consts, tb):
    b, t, n = pr.shape
    assert b <= SUBLANES
    rw = consts["w0"].shape[1]
    npair = rw // PAIR
    names = ["mu", "w0", "w2", "a0", "a2", "g2", "k_k", "k_a", "r_k", "lnx_g", "lnx_b",
             "bd", "tri", "strict", "incl", "eye"]
    ops = [consts[nm] for nm in names]
    full = lambda arr: pl.BlockSpec(arr.shape, lambda i: (0,) * arr.ndim)
    blk = lambda dt: pltpu.VMEM((b * tb, rw), dt)
    return pl.pallas_call(
        _rwkv_kernel,
        grid=(t // tb,),
        in_specs=[pl.BlockSpec((b, tb, n), lambda i: (0, i, 0))] + [full(x) for x in ops],
        out_specs=pl.BlockSpec((b, tb, rw), lambda i: (0, i, 0)),
        out_shape=jax.ShapeDtypeStruct((b, t, rw), BF16),
        scratch_shapes=[pltpu.VMEM((b * npair, PAIR, PAIR), F32), pltpu.VMEM((SUBLANES, n), F32)]
                       + [blk(BF16)] * 10 + [blk(F32)] * 4
                       + [pltpu.VMEM((b * tb // CHUNK, npair, PAIR, PAIR), BF16)] * 4,
        compiler_params=pltpu.CompilerParams(dimension_semantics=("arbitrary",),
                                             vmem_limit_bytes=VMEM_LIMIT),
        name="rwkv_group",
    )(pr, *ops)


def _outproj_kernel(u_ref, y_ref, x_ref, mod_ref, g_ref, gf_ref, w_ref, o_ref, h_ref):
    cw = u_ref.shape[2]
    m = _dot(u_ref[0], w_ref[0:cw, :]) + _dot(y_ref[0], w_ref[cw:, :])
    x1 = x_ref[0] + mod_ref[0, 2:3, :] * _rms(m, g_ref[...])
    o_ref[0] = x1
    h_ref[0] = (_rms(x1, gf_ref[...]) * (1.0 + mod_ref[0, 4:5, :]) + mod_ref[0, 3:4, :]).astype(h_ref.dtype)


def _outproj_call(u, y, x, mod, g, gf, w, tm):
    b, t, d = x.shape
    cw = u.shape[2]
    row_blk = lambda width: pl.BlockSpec((1, tm, width), lambda bi, i: (bi, i, 0))
    return pl.pallas_call(
        _outproj_kernel,
        grid=(b, t // tm),
        in_specs=[row_blk(cw), row_blk(y.shape[2]), row_blk(d),
                  pl.BlockSpec((1, 6, d), lambda bi, i: (bi, 0, 0)),
                  pl.BlockSpec((1, d), lambda bi, i: (0, 0)),
                  pl.BlockSpec((1, d), lambda bi, i: (0, 0)),
                  pl.BlockSpec(w.shape, lambda bi, i: (0, 0))],
        out_specs=[row_blk(d), row_blk(d)],
        out_shape=[jax.ShapeDtypeStruct((b, t, d), F32), jax.ShapeDtypeStruct((b, t, d), BF16)],
        compiler_params=pltpu.CompilerParams(dimension_semantics=("arbitrary", "arbitrary"),
                                             vmem_limit_bytes=VMEM_LIMIT),
        name="outproj",
    )(u, y, x, mod, g, gf, w)


def _ffn_kernel(x_ref, h_ref, halo_ref, mod_ref, gpost_ref, wup_ref, cw_ref, cb_ref, wdn_ref,
                o_ref, z_ref, acc_ref):
    i = pl.program_id(1)
    tm = x_ref.shape[1]
    d_ff = wdn_ref.shape[0]
    hb = jnp.concatenate([halo_ref[0], h_ref[0]], axis=0)
    keep = jnp.where(i > 0, 1.0, 0.0)
    n_chunks = d_ff // FF_CHUNK
    cols = lambda f: (slice(f * FF_CHUNK, (f + 1) * FF_CHUNK),
                      slice(d_ff + f * FF_CHUNK, d_ff + (f + 1) * FF_CHUNK))

    def up(f):
        cg, cv = cols(f)
        return jnp.concatenate([_dot(hb, wup_ref[:, cg]), _dot(hb, wup_ref[:, cv])], axis=1)

    def conv_act(f, z):
        cg, cv = cols(f)
        zb = z_ref.at[f % 2]
        zb[0:FFN_HALO, :] = z[0:FFN_HALO] * keep
        zb[FFN_HALO:, :] = z[FFN_HALO:]
        zc = jnp.concatenate([cb_ref[:, cg], cb_ref[:, cv]], axis=1)
        for k in range(FFN_K):
            o = FFN_HALO - (FFN_K - 1) + k
            wk = jnp.concatenate([cw_ref[k:k + 1, cg], cw_ref[k:k + 1, cv]], axis=1)
            zc = zc + wk * zb[o:o + tm, :]
        zg = zc[:, :FF_CHUNK]
        return (zg * _sigmoid(zg) * zc[:, FF_CHUNK:]).astype(BF16)

    act = conv_act(0, up(0))
    z_next = up(1)
    for f in range(n_chunks):
        z_cur = z_next
        if f + 2 < n_chunks:
            z_next = up(f + 2)
        contrib = _dot(act, wdn_ref[f * FF_CHUNK:(f + 1) * FF_CHUNK, :])
        if f == 0:
            acc_ref[...] = contrib
        else:
            acc_ref[...] += contrib
        if f + 1 < n_chunks:
            act = conv_act(f + 1, z_cur)
    o_ref[0] = x_ref[0] + mod_ref[0, 5:6, :] * _rms(acc_ref[...], gpost_ref[...])


def _ffn_call(x1, h2, mod, gpost, wup, cw, cb, wdn, tm):
    b, t, d = x1.shape
    hb = tm // FFN_HALO
    full = lambda arr: pl.BlockSpec(arr.shape, lambda bi, i: (0,) * arr.ndim)
    return pl.pallas_call(
        _ffn_kernel,
        grid=(b, t // tm),
        in_specs=[pl.BlockSpec((1, tm, d), lambda bi, i: (bi, i, 0)),
                  pl.BlockSpec((1, tm, d), lambda bi, i: (bi, i, 0)),
                  pl.BlockSpec((1, FFN_HALO, d), lambda bi, i: (bi, jnp.maximum(i * hb - 1, 0), 0)),
                  pl.BlockSpec((1, 6, d), lambda bi, i: (bi, 0, 0)),
                  full(gpost), full(wup), full(cw), full(cb), full(wdn)],
        out_specs=pl.BlockSpec((1, tm, d), lambda bi, i: (bi, i, 0)),
        out_shape=jax.ShapeDtypeStruct((b, t, d), F32),
        scratch_shapes=[pltpu.VMEM((2, tm + FFN_HALO, 2 * FF_CHUNK), F32),
                        pltpu.VMEM((tm, d), F32)],
        compiler_params=pltpu.CompilerParams(dimension_semantics=("arbitrary", "arbitrary"),
                                             vmem_limit_bytes=VMEM_LIMIT),
        name="conv_ffn",
    )(x1, h2, h2, mod, gpost, wup, cw, cb, wdn)


def _pad_cols(w, n):
    return jnp.pad(w, ((0, 0), (0, n - w.shape[1])))


def _pad_rows(w, n):
    return jnp.pad(w, ((0, n - w.shape[0]), (0, 0)))


def _layer(x, mod, lw, tiles):
    b, t, d = x.shape
    cw = lw["conv_dw_w"].shape[1]
    rw = lw["w0"].shape[0]
    dl = lw["w2"].shape[0]
    al = lw["a2"].shape[0]
    gl = lw["g2"].shape[0]
    row = lambda vec: vec.reshape(1, -1)

    o = 2 * cw + 3 * rw
    w_in = lw["w_in"].astype(BF16)
    w_in_r = jnp.concatenate(
        [w_in[:, :o], _pad_cols(w_in[:, o:o + dl], 128), _pad_cols(w_in[:, o + dl:o + dl + al], 128),
         _pad_cols(w_in[:, o + dl + al:], 256)], axis=1)
    mu = row(lw["rwkv_mu"])
    o2 = 3 * rw
    mu_r = jnp.concatenate([mu[:, :o2], _pad_cols(mu[:, o2:o2 + dl], 128),
                            _pad_cols(mu[:, o2 + dl:o2 + dl + al], 128),
                            _pad_cols(mu[:, o2 + dl + al:], 256)], axis=1)

    tb = tiles["rwkv"]
    ch = jnp.arange(rw) // HEAD
    rr = jnp.arange(tb)
    pp = jnp.arange(PAIR)
    consts = {
        "mu": mu_r,
        "w0": row(lw["w0"]), "w2": _pad_rows(lw["w2"], 128).astype(BF16),
        "a0": row(lw["a0"]), "a2": _pad_rows(lw["a2"], 128).astype(BF16),
        "g2": _pad_rows(lw["g2"], 256).astype(BF16),
        "k_k": row(lw["k_k"]), "k_a": row(lw["k_a"]), "r_k": row(lw["r_k"]),
        "lnx_g": row(lw["lnx_g"]), "lnx_b": row(lw["lnx_b"]),
        "bd": (ch[:, None] == ch[None, :]).astype(BF16),
        "tri": ((rr[:, None] // CHUNK == rr[None, :] // CHUNK) & (rr[None, :] <= rr[:, None])).astype(BF16),
        "strict": ((pp[None, :] % CHUNK) < (pp[:, None] % CHUNK)).astype(F32),
        "incl": ((pp[None, :] % CHUNK) <= (pp[:, None] % CHUNK)).astype(F32),
        "eye": jnp.eye(PAIR, dtype=F32),
    }

    pc, pr = _inproj_call(x, mod, row(lw["mix_pre_g"]), w_in_r, 2 * cw, tiles["inproj"])
    conv_w = jnp.broadcast_to(lw["conv_dw_w"][:, None, :], (CONV_K, SUBLANES, cw))
    u = _conv_call(pc, conv_w, row(lw["conv_dw_b"]), row(lw["conv_ln_g"]),
                   row(lw["conv_ln_b"]), tiles["conv"])
    y = _rwkv_call(pr, consts, tb)
    x1, h2 = _outproj_call(u, y, x, mod, row(lw["mix_post_g"]), row(lw["ffn_pre_g"]),
                           lw["w_out"].astype(BF16), tiles["outproj"])
    return _ffn_call(x1, h2, mod, row(lw["ffn_post_g"]), lw["w_up"].astype(BF16), lw["ffn_dw_w"],
                     row(lw["ffn_dw_b"]), lw["w_down"].astype(BF16), tiles["ffn"])


def _tiles(t):
    pick = lambda pref: min(pref, t)
    return {"inproj": pick(512), "conv": pick(256), "rwkv": pick(256), "outproj": pick(512),
            "ffn": pick(256)}


def kernel(x, c, ada_w, ada_b, mix_pre_g, mix_post_g, w_in, conv_dw_w, conv_dw_b, conv_ln_g,
           conv_ln_b, rwkv_mu, w0, w2, a0, a2, g2, k_k, k_a, r_k, lnx_g, lnx_b, w_out, ffn_pre_g,
           ffn_post_g, w_up, ffn_dw_w, ffn_dw_b, w_down):
    b, t, d = x.shape
    depth = ada_w.shape[0]
    tiles = _tiles(t)
    c8 = jnp.pad(c, ((0, 8 - b), (0, 0)))
    for l in range(depth):
        mod = _mod_call(c8, ada_w[l], ada_b[l].reshape(1, -1), 1536)[:b].reshape(b, 6, d)
        lw = {"mix_pre_g": mix_pre_g[l], "mix_post_g": mix_post_g[l], "w_in": w_in[l],
              "conv_dw_w": conv_dw_w[l], "conv_dw_b": conv_dw_b[l], "conv_ln_g": conv_ln_g[l],
              "conv_ln_b": conv_ln_b[l], "rwkv_mu": rwkv_mu[l], "w0": w0[l], "w2": w2[l],
              "a0": a0[l], "a2": a2[l], "g2": g2[l], "k_k": k_k[l], "k_a": k_a[l],
              "r_k": r_k[l].reshape(-1), "lnx_g": lnx_g[l], "lnx_b": lnx_b[l], "w_out": w_out[l],
              "ffn_pre_g": ffn_pre_g[l], "ffn_post_g": ffn_post_g[l], "w_up": w_up[l],
              "ffn_dw_w": ffn_dw_w[l], "ffn_dw_b": ffn_dw_b[l], "w_down": w_down[l]}
        x = _layer(x, mod, lw, tiles)
    return x
```

```python
import functools
import math

import jax
import jax.numpy as jnp
from jax import lax
from jax.experimental import pallas as pl
from jax.experimental.pallas import tpu as pltpu

F32 = jnp.float32
BF16 = jnp.bfloat16

HEAD = 64
CHUNK = 64
PAIR = 2 * HEAD
CONV_K = 31
FFN_K = 3
RMS_EPS = 1e-6
LN_EPS = 1e-5
LNX_EPS = 64e-5
SUBLANES = 8
CONV_HALO = 32
FFN_HALO = 16
FF_CHUNK = 256
A_GROUP = 2
SUB = 256
VMEM_LIMIT = 56 * 1024 * 1024

NT = (((1,), (1,)), ((), ()))
TN = (((0,), (0,)), ((), ()))


def _dot(a, b):
    return jnp.dot(a, b, preferred_element_type=F32)


def _dg(a, b, dims):
    return lax.dot_general(a, b, dims, preferred_element_type=F32)


def _sigmoid(x):
    return jax.nn.sigmoid(x)


def _rms(x, g):
    ms = jnp.mean(x * x, axis=-1, keepdims=True)
    return x * lax.rsqrt(ms + RMS_EPS) * g


def _mod_kernel(c_ref, w_ref, b_ref, o_ref):
    cs = c_ref[...]
    s = cs * _sigmoid(cs)
    o_ref[...] = jnp.dot(s, w_ref[...], preferred_element_type=F32,
                         precision=lax.Precision.HIGHEST) + b_ref[...]


def _mod_call(c8, ada_w, ada_b, tn):
    d, n = ada_w.shape
    return pl.pallas_call(
        _mod_kernel,
        grid=(n // tn,),
        in_specs=[pl.BlockSpec((8, d), lambda j: (0, 0)),
                  pl.BlockSpec((d, tn), lambda j: (0, j)),
                  pl.BlockSpec((1, tn), lambda j: (0, j))],
        out_specs=pl.BlockSpec((8, tn), lambda j: (0, j)),
        out_shape=jax.ShapeDtypeStruct((8, n), F32),
        compiler_params=pltpu.CompilerParams(dimension_semantics=("arbitrary",),
                                             vmem_limit_bytes=VMEM_LIMIT),
        name="adaln_mod",
    )(c8, ada_w, ada_b)


def _front_kernel(x_ref, mod_ref, g_ref, w_ref, mu_ref, w0_ref, w2_ref, a0_ref, a2_ref, g2_ref,
                  kk_ref, ka_ref, rk_ref, bd_ref, tri_ref,
                  pc_ref, kt_ref, rt_ref, kh_ref, bh_ref, v_ref, bonus_ref, gate_ref, el_ref,
                  prev_ref):
    i = pl.program_id(1)
    tm = x_ref.shape[1]
    nc = pc_ref.shape[2]
    rw = v_ref.shape[2]

    @pl.when(i == 0)
    def _():
        prev_ref[...] = jnp.zeros_like(prev_ref)

    x = x_ref[0]
    h = (_rms(x, g_ref[...]) * (1.0 + mod_ref[0, 1:2, :]) + mod_ref[0, 0:1, :]).astype(BF16)
    bd = bd_ref[...]
    tri = tri_ref[...]
    rows = lax.broadcasted_iota(jnp.int32, (SUB, 1), 0)

    def prep(p, r0):
        prev = jnp.where(rows == 0, prev_ref[0:1, :], pltpu.roll(p, 1, axis=0))
        prev_ref[0:1, :] = p[SUB - 1:SUB, :]
        xs = p + (prev - p) * mu_ref[...]
        r = xs[:, 0:rw]
        k = xs[:, rw:2 * rw]
        v = xs[:, 2 * rw:3 * rw]
        o = 3 * rw
        wd = xs[:, o:o + 128]
        ad = xs[:, o + 128:o + 256]
        gd = xs[:, o + 256:o + 512]
        zw = w0_ref[...] + _dot(jnp.tanh(wd).astype(BF16), w2_ref[...])
        lw = -math.exp(-0.5) * _sigmoid(zw)
        a = _sigmoid(a0_ref[...] + _dot(ad.astype(BF16), a2_ref[...]))
        gate = _dot(_sigmoid(gd).astype(BF16), g2_ref[...])
        kkr = k * kk_ref[...]
        ss = _dot((kkr * kkr).astype(BF16), bd)
        kk = kkr * lax.rsqrt(jnp.maximum(ss, 1e-24))
        km = k * (1.0 + (a - 1.0) * ka_ref[...])
        bonus = _dot((r * km * rk_ref[...]).astype(BF16), bd) * v
        hi = lw.astype(BF16)
        r1 = lw - hi.astype(F32)
        mid = r1.astype(BF16)
        lo = (r1 - mid.astype(F32)).astype(BF16)
        c = _dot(tri, hi) + _dot(tri, mid) + _dot(tri, lo)
        ec = jnp.exp(c)
        ecn = jnp.exp(-c)
        rs = slice(r0, r0 + SUB)
        kt_ref[0, rs, :] = (kk * jnp.exp(c - lw)).astype(BF16)
        rt_ref[0, rs, :] = (r * ec).astype(BF16)
        kh_ref[0, rs, :] = (km * ecn).astype(BF16)
        bh_ref[0, rs, :] = (a * kk * ecn).astype(BF16)
        v_ref[0, rs, :] = v.astype(BF16)
        bonus_ref[0, rs, :] = bonus.astype(BF16)
        gate_ref[0, rs, :] = gate.astype(BF16)
        for j in range(SUB // CHUNK):
            last = ec[(j + 1) * CHUNK - 1:(j + 1) * CHUNK, :]
            el_ref[0, r0 // CHUNK + j] = jnp.broadcast_to(last, (SUBLANES, rw))

    proj = lambda r0: _dot(h[r0:r0 + SUB], w_ref[...])
    p_next = proj(0)
    for r0 in range(0, tm, SUB):
        p = p_next
        if r0 + SUB < tm:
            p_next = proj(r0 + SUB)
        pc_ref[0, r0:r0 + SUB, :] = p[:, :nc]
        prep(p[:, nc:], r0)


def _front_call(x, mod, g, w, consts, n_conv, tm):
    b, t, d = x.shape
    n = w.shape[1]
    rw = consts["w0"].shape[1]
    names = ["mu", "w0", "w2", "a0", "a2", "g2", "k_k", "k_a", "r_k", "bd", "tri"]
    ops = [consts[nm] for nm in names]
    full = lambda arr: pl.BlockSpec(arr.shape, lambda bi, i: (0,) * arr.ndim)
    tok = lambda width: pl.BlockSpec((1, tm, width), lambda bi, i: (bi, i, 0))
    tok_shape = lambda width, dt: jax.ShapeDtypeStruct((b, t, width), dt)
    return pl.pallas_call(
        _front_kernel,
        grid=(b, t // tm),
        in_specs=[tok(d), pl.BlockSpec((1, 6, d), lambda bi, i: (bi, 0, 0)),
                  pl.BlockSpec((1, d), lambda bi, i: (0, 0)),
                  pl.BlockSpec((d, n), lambda bi, i: (0, 0))] + [full(a) for a in ops],
        out_specs=[tok(n_conv)] + [tok(rw)] * 7
                  + [pl.BlockSpec((1, tm // CHUNK, SUBLANES, rw), lambda bi, i: (bi, i, 0, 0))],
        out_shape=[tok_shape(n_conv, F32)] + [tok_shape(rw, BF16)] * 7
                  + [jax.ShapeDtypeStruct((b, t // CHUNK, SUBLANES, rw), F32)],
        scratch_shapes=[pltpu.VMEM((SUBLANES, n - n_conv), F32)],
        compiler_params=pltpu.CompilerParams(dimension_semantics=("arbitrary", "arbitrary"),
                                             vmem_limit_bytes=VMEM_LIMIT),
        name="inproj_prep",
    )(x, mod, g, w, *ops)


def _conv_kernel(pc_ref, halo_ref, w_ref, b_ref, lg_ref, lb_ref, o_ref, scr, shifted, *, rb):
    i = pl.program_id(1)
    tm = pc_ref.shape[1]
    cw = o_ref.shape[2]
    cur = pc_ref[0]
    scr[CONV_HALO:CONV_HALO + tm, :] = cur[:, :cw] * _sigmoid(cur[:, cw:])
    hal = halo_ref[0]
    gh = hal[:, :cw] * _sigmoid(hal[:, cw:])
    scr[0:CONV_HALO, :] = jnp.where(i > 0, gh, 0.0)
    span = tm + CONV_HALO - SUBLANES
    for r in range(1, SUBLANES):
        shifted[r - 1, 0:span, :] = scr[r:r + span, :]
    off = CONV_HALO - (CONV_K - 1)
    for r0 in range(0, tm, rb):
        acc = jnp.zeros((rb // SUBLANES, SUBLANES, cw), F32) + b_ref[...]
        for k in range(CONV_K):
            r = (off + k) % SUBLANES
            a0 = r0 + off + k - r
            src = scr[a0:a0 + rb, :] if r == 0 else shifted[r - 1, a0:a0 + rb, :]
            acc = acc + w_ref[k] * src.reshape(rb // SUBLANES, SUBLANES, cw)
        acc = acc.reshape(rb, cw)
        mu = jnp.mean(acc, axis=-1, keepdims=True)
        dd = acc - mu
        var = jnp.mean(dd * dd, axis=-1, keepdims=True)
        z = dd * lax.rsqrt(var + LN_EPS) * lg_ref[...] + lb_ref[...]
        o_ref[0, r0:r0 + rb, :] = (z * _sigmoid(z)).astype(o_ref.dtype)


def _conv_call(pc, w, bias, lg, lb, tm, rb=32):
    b, t, n2 = pc.shape
    cw = n2 // 2
    hb = tm // CONV_HALO
    return pl.pallas_call(
        functools.partial(_conv_kernel, rb=rb),
        grid=(b, t // tm),
        in_specs=[pl.BlockSpec((1, tm, n2), lambda bi, i: (bi, i, 0)),
                  pl.BlockSpec((1, CONV_HALO, n2), lambda bi, i: (bi, jnp.maximum(i * hb - 1, 0), 0)),
                  pl.BlockSpec((CONV_K, SUBLANES, cw), lambda bi, i: (0, 0, 0)),
                  pl.BlockSpec((1, cw), lambda bi, i: (0, 0)),
                  pl.BlockSpec((1, cw), lambda bi, i: (0, 0)),
                  pl.BlockSpec((1, cw), lambda bi, i: (0, 0))],
        out_specs=pl.BlockSpec((1, tm, cw), lambda bi, i: (bi, i, 0)),
        out_shape=jax.ShapeDtypeStruct((b, t, cw), BF16),
        scratch_shapes=[pltpu.VMEM((tm + CONV_HALO, cw), F32),
                        pltpu.VMEM((SUBLANES - 1, tm + CONV_HALO - SUBLANES, cw), F32)],
        compiler_params=pltpu.CompilerParams(dimension_semantics=("arbitrary", "arbitrary"),
                                             vmem_limit_bytes=VMEM_LIMIT),
        name="conv_group",
    )(pc, pc, w, bias, lg, lb)


def _stack(val, m0, m1):
    return jnp.concatenate([val * m0, val * m1], axis=0)


def _neumann_inverse(n_w, eye_w, m0, m1):
    levels = CHUNK.bit_length() - 2
    t_w = [eye_w + n_ for n_ in n_w]
    pb = [n_.astype(BF16) for n_ in n_w]
    pk = [_dot(x, _stack(x, m0, m1)) for x in pb]
    for k in range(1, levels + 1):
        pkb = [x.astype(BF16) for x in pk]
        pd = [_stack(x, m0, m1) for x in pkb]
        if k < levels:
            both = [_dot(jnp.concatenate([t_.astype(BF16), p_], axis=0), d_)
                    for t_, p_, d_ in zip(t_w, pkb, pd)]
            t_w = [t_ + b_[:CHUNK] for t_, b_ in zip(t_w, both)]
            pk = [b_[CHUNK:] for b_ in both]
        else:
            t_w = [t_ + _dot(t_.astype(BF16), d_) for t_, d_ in zip(t_w, pd)]
    return t_w


def _rwkv_kernel(kt_ref, rt_ref, kh_ref, bh_ref, v_ref, bonus_ref, gate_ref, el_ref,
                 lg_ref, lb_ref, bd_ref, hmask_ref, strict_ref, incl_ref, eye_ref,
                 o_ref,
                 s_ref, y_ref, ak_ref, abr_ref, t_ref):
    i = pl.program_id(0)
    nb, tb, rw = o_ref.shape
    npair = rw // PAIR
    cpb = tb // CHUNK

    @pl.when(i == 0)
    def _():
        s_ref[...] = jnp.zeros_like(s_ref)

    strict = strict_ref[...]
    incl = incl_ref[...]
    eye = eye_ref[...]
    m0 = hmask_ref[0]
    m1 = hmask_ref[1]
    pairs = range(npair)
    lanes = [slice(pi * PAIR, (pi + 1) * PAIR) for pi in pairs]
    stack = lambda val: _stack(val, m0, m1)

    def rows_of(j):
        return pl.ds(pl.multiple_of(j * CHUNK, CHUNK), CHUNK)

    def phase_a(gi, carry):
        inst = [(bi, gi * A_GROUP + j, pi) for bi in range(nb) for j in range(A_GROUP) for pi in pairs]
        sl = [(bi, rows_of(j), lanes[pi]) for bi, j, pi in inst]
        lhs = [jnp.concatenate([kt_ref[x], rt_ref[x]], axis=0) for x in sl]
        kb = [jnp.concatenate([stack(kh_ref[x]), stack(bh_ref[x])], axis=0) for x in sl]
        g = [_dg(l_, k_, NT) for l_, k_ in zip(lhs, kb)]
        for g_, (bi, j, pi) in zip(g, inst):
            ci = bi * cpb + j
            ak_ref[ci, pi, 0:CHUNK] = (g_[0:CHUNK, 0:PAIR] * strict).astype(BF16)
            ak_ref[ci, pi, CHUNK:] = (g_[CHUNK:, 0:PAIR] * incl).astype(BF16)
            abr_ref[ci, pi] = (g_[CHUNK:, PAIR:] * incl).astype(BF16)
        t_w = _neumann_inverse([-(g_[0:CHUNK, PAIR:] * strict) for g_ in g], eye, m0, m1)
        for t_, (bi, j, pi) in zip(t_w, inst):
            t_ref[bi * cpb + j, pi] = t_.astype(BF16)
        return carry

    lax.fori_loop(0, cpb // A_GROUP, phase_a, 0)

    def phase_b(j, carry):
        chains = [(bi, pi) for bi in range(nb) for pi in pairs]
        cis = [bi * cpb + j for bi, _ in chains]
        rs = rows_of(j)
        sl = [(bi, rs, lanes[pi]) for bi, pi in chains]
        lhs = [jnp.concatenate([kt_ref[x], rt_ref[x]], axis=0) for x in sl]
        kb = [jnp.concatenate([stack(kh_ref[x]), -stack(bh_ref[x])], axis=0) for x in sl]
        vs = [stack(v_ref[x]) for x in sl]
        s = [s_ref[bi * npair + pi] for bi, pi in chains]
        sb = [x.astype(BF16) for x in s]
        xy = [_dg(l_, sb_, NT) + _dot(ak_ref[ci, pi], v_)
              for l_, sb_, v_, ci, (_, pi) in zip(lhs, sb, vs, cis, chains)]
        us = [stack(_dot(t_ref[ci, pi], stack(x_[0:CHUNK].astype(BF16))).astype(BF16))
              for x_, ci, (_, pi) in zip(xy, cis, chains)]
        ds_ = [_dg(jnp.concatenate([v_, u_], axis=0), k_, TN) for v_, u_, k_ in zip(vs, us, kb)]
        for s_, d_, (bi, pi) in zip(s, ds_, chains):
            e_l = el_ref[bi, j][0:1, lanes[pi]]
            s_ref[bi * npair + pi] = (s_ + d_) * e_l
        for x_, u_, x, ci, (_, pi) in zip(xy, us, sl, cis, chains):
            y_ref[x] = x_[CHUNK:] - _dot(abr_ref[ci, pi], u_)
        return carry

    lax.fori_loop(0, cpb, phase_b, 0, unroll=2)

    bd = bd_ref[...]
    y = y_ref[...].reshape(nb * tb, rw)
    inv = 1.0 / HEAD
    mean = _dot(y.astype(BF16), bd) * inv
    d = y - mean
    var = _dot((d * d).astype(BF16), bd) * inv
    yl = d * lax.rsqrt(var + LNX_EPS) * lg_ref[...] + lb_ref[...]
    bonus = bonus_ref[...].reshape(nb * tb, rw).astype(F32)
    gate = gate_ref[...].reshape(nb * tb, rw).astype(F32)
    o_ref[...] = ((yl + bonus) * gate).astype(o_ref.dtype).reshape(nb, tb, rw)


def _rwkv_call(streams, el, consts, tb):
    b, t, rw = streams[0].shape
    npair = rw // PAIR
    names = ["lnx_g", "lnx_b", "bd", "hmask", "strict", "incl", "eye"]
    ops = [consts[nm] for nm in names]
    full = lambda arr: pl.BlockSpec(arr.shape, lambda i: (0,) * arr.ndim)
    tok = pl.BlockSpec((b, tb, rw), lambda i: (0, i, 0))
    return pl.pallas_call(
        _rwkv_kernel,
        grid=(t // tb,),
        in_specs=[tok] * len(streams)
                 + [pl.BlockSpec((b, tb // CHUNK, SUBLANES, rw), lambda i: (0, i, 0, 0))]
                 + [full(a) for a in ops],
        out_specs=tok,
        out_shape=jax.ShapeDtypeStruct((b, t, rw), BF16),
        scratch_shapes=[pltpu.VMEM((b * npair, PAIR, PAIR), F32), pltpu.VMEM((b, tb, rw), F32)]
                       + [pltpu.VMEM((b * tb // CHUNK, npair, 2 * CHUNK, PAIR), BF16)]
                       + [pltpu.VMEM((b * tb // CHUNK, npair, CHUNK, PAIR), BF16)] * 2,
        compiler_params=pltpu.CompilerParams(dimension_semantics=("arbitrary",),
                                             vmem_limit_bytes=VMEM_LIMIT),
        name="rwkv_group",
    )(*streams, el, *ops)


def _outproj_kernel(u_ref, y_ref, x_ref, mod_ref, g_ref, gf_ref, w_ref, o_ref, h_ref):
    cw = u_ref.shape[2]
    m = _dot(u_ref[0], w_ref[0:cw, :]) + _dot(y_ref[0], w_ref[cw:, :])
    x1 = x_ref[0] + mod_ref[0, 2:3, :] * _rms(m, g_ref[...])
    o_ref[0] = x1
    h_ref[0] = (_rms(x1, gf_ref[...]) * (1.0 + mod_ref[0, 4:5, :]) + mod_ref[0, 3:4, :]).astype(h_ref.dtype)


def _outproj_call(u, y, x, mod, g, gf, w, tm):
    b, t, d = x.shape
    cw = u.shape[2]
    row_blk = lambda width: pl.BlockSpec((1, tm, width), lambda bi, i: (bi, i, 0))
    return pl.pallas_call(
        _outproj_kernel,
        grid=(b, t // tm),
        in_specs=[row_blk(cw), row_blk(y.shape[2]), row_blk(d),
                  pl.BlockSpec((1, 6, d), lambda bi, i: (bi, 0, 0)),
                  pl.BlockSpec((1, d), lambda bi, i: (0, 0)),
                  pl.BlockSpec((1, d), lambda bi, i: (0, 0)),
                  pl.BlockSpec(w.shape, lambda bi, i: (0, 0))],
        out_specs=[row_blk(d), row_blk(d)],
        out_shape=[jax.ShapeDtypeStruct((b, t, d), F32), jax.ShapeDtypeStruct((b, t, d), BF16)],
        compiler_params=pltpu.CompilerParams(dimension_semantics=("arbitrary", "arbitrary"),
                                             vmem_limit_bytes=VMEM_LIMIT),
        name="outproj",
    )(u, y, x, mod, g, gf, w)


def _ffn_kernel(x_ref, h_ref, halo_ref, mod_ref, gpost_ref, wup_ref, cw_ref, cb_ref, wdn_ref,
                o_ref, z_ref, acc_ref):
    i = pl.program_id(1)
    tm = x_ref.shape[1]
    d_ff = wdn_ref.shape[0]
    hb = jnp.concatenate([halo_ref[0], h_ref[0]], axis=0)
    keep = jnp.where(i > 0, 1.0, 0.0)
    n_chunks = d_ff // FF_CHUNK
    cols = lambda f: (slice(f * FF_CHUNK, (f + 1) * FF_CHUNK),
                      slice(d_ff + f * FF_CHUNK, d_ff + (f + 1) * FF_CHUNK))

    def up(f):
        cg, cv = cols(f)
        return jnp.concatenate([_dot(hb, wup_ref[:, cg]), _dot(hb, wup_ref[:, cv])], axis=1)

    def conv_act(f, z):
        cg, cv = cols(f)
        zb = z_ref.at[f % 2]
        zb[0:FFN_HALO, :] = z[0:FFN_HALO] * keep
        zb[FFN_HALO:, :] = z[FFN_HALO:]
        zc = jnp.concatenate([cb_ref[:, cg], cb_ref[:, cv]], axis=1)
        for k in range(FFN_K):
            o = FFN_HALO - (FFN_K - 1) + k
            wk = jnp.concatenate([cw_ref[k:k + 1, cg], cw_ref[k:k + 1, cv]], axis=1)
            zc = zc + wk * zb[o:o + tm, :]
        zg = zc[:, :FF_CHUNK]
        return (zg * _sigmoid(zg) * zc[:, FF_CHUNK:]).astype(BF16)

    act = conv_act(0, up(0))
    z_next = up(1)
    for f in range(n_chunks):
        z_cur = z_next
        if f + 2 < n_chunks:
            z_next = up(f + 2)
        contrib = _dot(act, wdn_ref[f * FF_CHUNK:(f + 1) * FF_CHUNK, :])
        if f == 0:
            acc_ref[...] = contrib
        else:
            acc_ref[...] += contrib
        if f + 1 < n_chunks:
            act = conv_act(f + 1, z_cur)
    o_ref[0] = x_ref[0] + mod_ref[0, 5:6, :] * _rms(acc_ref[...], gpost_ref[...])


def _ffn_call(x1, h2, mod, gpost, wup, cw, cb, wdn, tm):
    b, t, d = x1.shape
    hb = tm // FFN_HALO
    full = lambda arr: pl.BlockSpec(arr.shape, lambda bi, i: (0,) * arr.ndim)
    return pl.pallas_call(
        _ffn_kernel,
        grid=(b, t // tm),
        in_specs=[pl.BlockSpec((1, tm, d), lambda bi, i: (bi, i, 0)),
                  pl.BlockSpec((1, tm, d), lambda bi, i: (bi, i, 0)),
                  pl.BlockSpec((1, FFN_HALO, d), lambda bi, i: (bi, jnp.maximum(i * hb - 1, 0), 0)),
                  pl.BlockSpec((1, 6, d), lambda bi, i: (bi, 0, 0)),
                  full(gpost), full(wup), full(cw), full(cb), full(wdn)],
        out_specs=pl.BlockSpec((1, tm, d), lambda bi, i: (bi, i, 0)),
        out_shape=jax.ShapeDtypeStruct((b, t, d), F32),
        scratch_shapes=[pltpu.VMEM((2, tm + FFN_HALO, 2 * FF_CHUNK), F32),
                        pltpu.VMEM((tm, d), F32)],
        compiler_params=pltpu.CompilerParams(dimension_semantics=("arbitrary", "arbitrary"),
                                             vmem_limit_bytes=VMEM_LIMIT),
        name="conv_ffn",
    )(x1, h2, h2, mod, gpost, wup, cw, cb, wdn)


def _pad_cols(w, n):
    return jnp.pad(w, ((0, 0), (0, n - w.shape[1])))


def _pad_rows(w, n):
    return jnp.pad(w, ((0, n - w.shape[0]), (0, 0)))


def _layer(x, mod, lw, tiles):
    b, t, d = x.shape
    cw = lw["conv_dw_w"].shape[1]
    rw = lw["w0"].shape[0]
    dl = lw["w2"].shape[0]
    al = lw["a2"].shape[0]
    row = lambda vec: vec.reshape(1, -1)

    o = 2 * cw + 3 * rw
    w_in = lw["w_in"].astype(BF16)
    w_in_r = jnp.concatenate(
        [w_in[:, :o], _pad_cols(w_in[:, o:o + dl], 128), _pad_cols(w_in[:, o + dl:o + dl + al], 128),
         _pad_cols(w_in[:, o + dl + al:], 256)], axis=1)
    mu = row(lw["rwkv_mu"])
    o2 = 3 * rw
    mu_r = jnp.concatenate([mu[:, :o2], _pad_cols(mu[:, o2:o2 + dl], 128),
                            _pad_cols(mu[:, o2 + dl:o2 + dl + al], 128),
                            _pad_cols(mu[:, o2 + dl + al:], 256)], axis=1)

    ch = jnp.arange(rw) // HEAD
    rr = jnp.arange(SUB)
    pp = jnp.arange(PAIR)
    tt = jnp.arange(CHUNK)
    first = (pp < HEAD).astype(BF16)
    consts = {
        "mu": mu_r,
        "w0": row(lw["w0"]), "w2": _pad_rows(lw["w2"], 128).astype(BF16),
        "a0": row(lw["a0"]), "a2": _pad_rows(lw["a2"], 128).astype(BF16),
        "g2": _pad_rows(lw["g2"], 256).astype(BF16),
        "k_k": row(lw["k_k"]), "k_a": row(lw["k_a"]), "r_k": row(lw["r_k"]),
        "lnx_g": row(lw["lnx_g"]), "lnx_b": row(lw["lnx_b"]),
        "bd": (ch[:, None] == ch[None, :]).astype(BF16),
        "tri": ((rr[:, None] // CHUNK == rr[None, :] // CHUNK) & (rr[None, :] <= rr[:, None])).astype(BF16),
        "hmask": jnp.broadcast_to(jnp.stack([first, 1 - first])[:, None, :], (2, CHUNK, PAIR)),
        "strict": ((pp[None, :] % CHUNK) < tt[:, None]).astype(F32),
        "incl": ((pp[None, :] % CHUNK) <= tt[:, None]).astype(F32),
        "eye": ((pp[None, :] % CHUNK) == tt[:, None]).astype(F32),
    }

    pc, *streams, el = _front_call(x, mod, row(lw["mix_pre_g"]), w_in_r, consts, 2 * cw, tiles["inproj"])
    conv_w = jnp.broadcast_to(lw["conv_dw_w"][:, None, :], (CONV_K, SUBLANES, cw))
    u = _conv_call(pc, conv_w, row(lw["conv_dw_b"]), row(lw["conv_ln_g"]),
                   row(lw["conv_ln_b"]), tiles["conv"])
    y = _rwkv_call(streams, el, consts, tiles["rwkv"])
    x1, h2 = _outproj_call(u, y, x, mod, row(lw["mix_post_g"]), row(lw["ffn_pre_g"]),
                           lw["w_out"].astype(BF16), tiles["outproj"])
    return _ffn_call(x1, h2, mod, row(lw["ffn_post_g"]), lw["w_up"].astype(BF16), lw["ffn_dw_w"],
                     row(lw["ffn_dw_b"]), lw["w_down"].astype(BF16), tiles["ffn"])


def _tiles(t):
    pick = lambda pref: min(pref, t)
    return {"inproj": pick(512), "conv": pick(256), "rwkv": pick(512), "outproj": pick(512),
            "ffn": pick(256)}


def kernel(x, c, ada_w, ada_b, mix_pre_g, mix_post_g, w_in, conv_dw_w, conv_dw_b, conv_ln_g,
           conv_ln_b, rwkv_mu, w0, w2, a0, a2, g2, k_k, k_a, r_k, lnx_g, lnx_b, w_out, ffn_pre_g,
           ffn_post_g, w_up, ffn_dw_w, ffn_dw_b, w_down):
    b, t, d = x.shape
    depth = ada_w.shape[0]
    tiles = _tiles(t)
    c8 = jnp.pad(c, ((0, 8 - b), (0, 0)))
    for l in range(depth):
        mod = _mod_call(c8, ada_w[l], ada_b[l].reshape(1, -1), 1536)[:b].reshape(b, 6, d)
        lw = {"mix_pre_g": mix_pre_g[l], "mix_post_g": mix_post_g[l], "w_in": w_in[l],
              "conv_dw_w": conv_dw_w[l], "conv_dw_b": conv_dw_b[l], "conv_ln_g": conv_ln_g[l],
              "conv_ln_b": conv_ln_b[l], "rwkv_mu": rwkv_mu[l], "w0": w0[l], "w2": w2[l],
              "a0": a0[l], "a2": a2[l], "g2": g2[l], "k_k": k_k[l], "k_a": k_a[l],
              "r_k": r_k[l].reshape(-1), "lnx_g": lnx_g[l], "lnx_b": lnx_b[l], "w_out": w_out[l],
              "ffn_pre_g": ffn_pre_g[l], "ffn_post_g": ffn_post_g[l], "w_up": w_up[l],
              "ffn_dw_w": ffn_dw_w[l], "ffn_dw_b": ffn_dw_b[l], "w_down": w_down[l]}
        x = _layer(x, mod, lw, tiles)
    return x
```

```python
import functools
import math

import jax
import jax.numpy as jnp
from jax import lax
from jax.experimental import pallas as pl
from jax.experimental.pallas import tpu as pltpu

F32 = jnp.float32
BF16 = jnp.bfloat16

HEAD = 64
CHUNK = 64
PAIR = 2 * HEAD
CONV_K = 31
FFN_K = 3
RMS_EPS = 1e-6
LN_EPS = 1e-5
LNX_EPS = 64e-5
SUBLANES = 8
CONV_HALO = 32
FF_CHUNK = 256
A_GROUP = 2
SUB = 256
VMEM_LIMIT = 56 * 1024 * 1024

NT = (((1,), (1,)), ((), ()))
TN = (((0,), (0,)), ((), ()))


def _dot(a, b):
    return jnp.dot(a, b, preferred_element_type=F32)


def _dg(a, b, dims):
    return lax.dot_general(a, b, dims, preferred_element_type=F32)


def _sigmoid(x):
    return jax.nn.sigmoid(x)


def _rms(x, g):
    ms = jnp.mean(x * x, axis=-1, keepdims=True)
    return x * lax.rsqrt(ms + RMS_EPS) * g


def _mod_kernel(c_ref, w_ref, b_ref, o_ref):
    cs = c_ref[...]
    s = cs * _sigmoid(cs)
    o_ref[...] = jnp.dot(s, w_ref[...], preferred_element_type=F32,
                         precision=lax.Precision.HIGHEST) + b_ref[...]


def _mod_call(c8, ada_w, ada_b, tn):
    d, n = ada_w.shape
    return pl.pallas_call(
        _mod_kernel,
        grid=(n // tn,),
        in_specs=[pl.BlockSpec((8, d), lambda j: (0, 0)),
                  pl.BlockSpec((d, tn), lambda j: (0, j)),
                  pl.BlockSpec((1, tn), lambda j: (0, j))],
        out_specs=pl.BlockSpec((8, tn), lambda j: (0, j)),
        out_shape=jax.ShapeDtypeStruct((8, n), F32),
        compiler_params=pltpu.CompilerParams(dimension_semantics=("arbitrary",),
                                             vmem_limit_bytes=VMEM_LIMIT),
        name="adaln_mod",
    )(c8, ada_w, ada_b)


def _front_kernel(x_ref, mod_ref, g_ref, w_ref, mu_ref, w0_ref, w2_ref, a0_ref, a2_ref, g2_ref,
                  kk_ref, ka_ref, rk_ref, bd_ref, tri_ref,
                  pc_ref, kt_ref, rt_ref, kh_ref, bh_ref, v_ref, bonus_ref, gate_ref, el_ref,
                  prev_ref):
    i = pl.program_id(1)
    tm = x_ref.shape[1]
    nc = pc_ref.shape[2]
    rw = v_ref.shape[2]

    @pl.when(i == 0)
    def _():
        prev_ref[...] = jnp.zeros_like(prev_ref)

    x = x_ref[0]
    h = (_rms(x, g_ref[...]) * (1.0 + mod_ref[0, 1:2, :]) + mod_ref[0, 0:1, :]).astype(BF16)
    bd = bd_ref[...]
    tri = tri_ref[...]
    rows = lax.broadcasted_iota(jnp.int32, (SUB, 1), 0)

    def prep(p, r0):
        prev = jnp.where(rows == 0, prev_ref[0:1, :], pltpu.roll(p, 1, axis=0))
        prev_ref[0:1, :] = p[SUB - 1:SUB, :]
        xs = p + (prev - p) * mu_ref[...]
        r = xs[:, 0:rw]
        k = xs[:, rw:2 * rw]
        v = xs[:, 2 * rw:3 * rw]
        o = 3 * rw
        wd = xs[:, o:o + 128]
        ad = xs[:, o + 128:o + 256]
        gd = xs[:, o + 256:o + 512]
        zw = w0_ref[...] + _dot(jnp.tanh(wd).astype(BF16), w2_ref[...])
        lw = -math.exp(-0.5) * _sigmoid(zw)
        a = _sigmoid(a0_ref[...] + _dot(ad.astype(BF16), a2_ref[...]))
        gate = _dot(_sigmoid(gd).astype(BF16), g2_ref[...])
        kkr = k * kk_ref[...]
        ss = _dot((kkr * kkr).astype(BF16), bd)
        kk = kkr * lax.rsqrt(jnp.maximum(ss, 1e-24))
        km = k * (1.0 + (a - 1.0) * ka_ref[...])
        bonus = _dot((r * km * rk_ref[...]).astype(BF16), bd) * v
        hi = lw.astype(BF16)
        r1 = lw - hi.astype(F32)
        mid = r1.astype(BF16)
        lo = (r1 - mid.astype(F32)).astype(BF16)
        c = _dot(tri, hi) + _dot(tri, mid) + _dot(tri, lo)
        ec = jnp.exp(c)
        ecn = jnp.exp(-c)
        rs = slice(r0, r0 + SUB)
        kt_ref[0, rs, :] = (kk * jnp.exp(c - lw)).astype(BF16)
        rt_ref[0, rs, :] = (r * ec).astype(BF16)
        kh_ref[0, rs, :] = (km * ecn).astype(BF16)
        bh_ref[0, rs, :] = (a * kk * ecn).astype(BF16)
        v_ref[0, rs, :] = v.astype(BF16)
        bonus_ref[0, rs, :] = bonus.astype(BF16)
        gate_ref[0, rs, :] = gate.astype(BF16)
        for j in range(SUB // CHUNK):
            last = ec[(j + 1) * CHUNK - 1:(j + 1) * CHUNK, :]
            el_ref[0, r0 // CHUNK + j] = jnp.broadcast_to(last, (SUBLANES, rw))

    proj = lambda r0: _dot(h[r0:r0 + SUB], w_ref[...])
    p_next = proj(0)
    for r0 in range(0, tm, SUB):
        p = p_next
        if r0 + SUB < tm:
            p_next = proj(r0 + SUB)
        pc_ref[0, r0:r0 + SUB, :] = p[:, :nc]
        prep(p[:, nc:], r0)


def _front_call(x, mod, g, w, consts, n_conv, tm):
    b, t, d = x.shape
    n = w.shape[1]
    rw = consts["w0"].shape[1]
    names = ["mu", "w0", "w2", "a0", "a2", "g2", "k_k", "k_a", "r_k", "bd", "tri"]
    ops = [consts[nm] for nm in names]
    full = lambda arr: pl.BlockSpec(arr.shape, lambda bi, i: (0,) * arr.ndim)
    tok = lambda width: pl.BlockSpec((1, tm, width), lambda bi, i: (bi, i, 0))
    tok_shape = lambda width, dt: jax.ShapeDtypeStruct((b, t, width), dt)
    return pl.pallas_call(
        _front_kernel,
        grid=(b, t // tm),
        in_specs=[tok(d), pl.BlockSpec((1, 6, d), lambda bi, i: (bi, 0, 0)),
                  pl.BlockSpec((1, d), lambda bi, i: (0, 0)),
                  pl.BlockSpec((d, n), lambda bi, i: (0, 0))] + [full(a) for a in ops],
        out_specs=[tok(n_conv)] + [tok(rw)] * 7
                  + [pl.BlockSpec((1, tm // CHUNK, SUBLANES, rw), lambda bi, i: (bi, i, 0, 0))],
        out_shape=[tok_shape(n_conv, F32)] + [tok_shape(rw, BF16)] * 7
                  + [jax.ShapeDtypeStruct((b, t // CHUNK, SUBLANES, rw), F32)],
        scratch_shapes=[pltpu.VMEM((SUBLANES, n - n_conv), F32)],
        compiler_params=pltpu.CompilerParams(dimension_semantics=("arbitrary", "arbitrary"),
                                             vmem_limit_bytes=VMEM_LIMIT),
        name="inproj_prep",
    )(x, mod, g, w, *ops)


def _mixout_kernel(pc_ref, halo_ref, cwt_ref, cb_ref, lg_ref, lb_ref, y_ref, x_ref, mod_ref, g_ref, gf_ref,
                   w_ref, o_ref, h_ref, scr, shifted, u_scr, my_scr, *, rb):
    i = pl.program_id(1)
    tm = pc_ref.shape[1]
    cw = y_ref.shape[2]
    my_scr[...] = _dot(y_ref[0], w_ref[cw:, :])

    cur = pc_ref[0]
    scr[CONV_HALO:CONV_HALO + tm, :] = cur[:, :cw] * _sigmoid(cur[:, cw:])
    hal = halo_ref[0]
    gh = hal[:, :cw] * _sigmoid(hal[:, cw:])
    scr[0:CONV_HALO, :] = jnp.where(i > 0, gh, 0.0)
    span = tm + CONV_HALO - SUBLANES
    for r in range(1, SUBLANES):
        shifted[r - 1, 0:span, :] = scr[r:r + span, :]
    off = CONV_HALO - (CONV_K - 1)
    for r0 in range(0, tm, rb):
        acc = jnp.zeros((rb // SUBLANES, SUBLANES, cw), F32) + cb_ref[...]
        for k in range(CONV_K):
            r = (off + k) % SUBLANES
            a0 = r0 + off + k - r
            src = scr[a0:a0 + rb, :] if r == 0 else shifted[r - 1, a0:a0 + rb, :]
            acc = acc + cwt_ref[k] * src.reshape(rb // SUBLANES, SUBLANES, cw)
        acc = acc.reshape(rb, cw)
        mu = jnp.mean(acc, axis=-1, keepdims=True)
        dd = acc - mu
        var = jnp.mean(dd * dd, axis=-1, keepdims=True)
        z = dd * lax.rsqrt(var + LN_EPS) * lg_ref[...] + lb_ref[...]
        u_scr[r0:r0 + rb, :] = (z * _sigmoid(z)).astype(BF16)

    m = _dot(u_scr[...], w_ref[0:cw, :]) + my_scr[...]
    x1 = x_ref[0] + mod_ref[0, 2:3, :] * _rms(m, g_ref[...])
    o_ref[0] = x1
    h_ref[0] = (_rms(x1, gf_ref[...]) * (1.0 + mod_ref[0, 4:5, :]) + mod_ref[0, 3:4, :]).astype(h_ref.dtype)


def _mixout_call(pc, conv_w, conv_b, lg, lb, y, x, mod, g, gf, w, tm, rb=32):
    b, t, d = x.shape
    n2 = pc.shape[2]
    cw = n2 // 2
    hb = tm // CONV_HALO
    row_blk = lambda width: pl.BlockSpec((1, tm, width), lambda bi, i: (bi, i, 0))
    vec = lambda width: pl.BlockSpec((1, width), lambda bi, i: (0, 0))
    return pl.pallas_call(
        functools.partial(_mixout_kernel, rb=rb),
        grid=(b, t // tm),
        in_specs=[row_blk(n2),
                  pl.BlockSpec((1, CONV_HALO, n2), lambda bi, i: (bi, jnp.maximum(i * hb - 1, 0), 0)),
                  pl.BlockSpec((CONV_K, SUBLANES, cw), lambda bi, i: (0, 0, 0)),
                  vec(cw), vec(cw), vec(cw),
                  row_blk(y.shape[2]), row_blk(d),
                  pl.BlockSpec((1, 6, d), lambda bi, i: (bi, 0, 0)),
                  vec(d), vec(d),
                  pl.BlockSpec(w.shape, lambda bi, i: (0, 0))],
        out_specs=[row_blk(d), row_blk(d)],
        out_shape=[jax.ShapeDtypeStruct((b, t, d), F32), jax.ShapeDtypeStruct((b, t, d), BF16)],
        scratch_shapes=[pltpu.VMEM((tm + CONV_HALO, cw), F32),
                        pltpu.VMEM((SUBLANES - 1, tm + CONV_HALO - SUBLANES, cw), F32),
                        pltpu.VMEM((tm, cw), BF16),
                        pltpu.VMEM((tm, d), F32)],
        compiler_params=pltpu.CompilerParams(dimension_semantics=("arbitrary", "arbitrary"),
                                             vmem_limit_bytes=VMEM_LIMIT),
        name="conv_outproj",
    )(pc, pc, conv_w, conv_b, lg, lb, y, x, mod, g, gf, w)


def _stack(val, m0, m1):
    return jnp.concatenate([val * m0, val * m1], axis=0)


def _neumann_inverse(n_w, eye_w, m0, m1):
    levels = CHUNK.bit_length() - 2
    t_w = [eye_w + n_ for n_ in n_w]
    pb = [n_.astype(BF16) for n_ in n_w]
    pk = [_dot(x, _stack(x, m0, m1)) for x in pb]
    for k in range(1, levels + 1):
        pkb = [x.astype(BF16) for x in pk]
        pd = [_stack(x, m0, m1) for x in pkb]
        if k < levels:
            both = [_dot(jnp.concatenate([t_.astype(BF16), p_], axis=0), d_)
                    for t_, p_, d_ in zip(t_w, pkb, pd)]
            t_w = [t_ + b_[:CHUNK] for t_, b_ in zip(t_w, both)]
            pk = [b_[CHUNK:] for b_ in both]
        else:
            t_w = [t_ + _dot(t_.astype(BF16), d_) for t_, d_ in zip(t_w, pd)]
    return t_w


def _rwkv_kernel(kt_ref, rt_ref, kh_ref, bh_ref, v_ref, bonus_ref, gate_ref, el_ref,
                 lg_ref, lb_ref, bd_ref, hmask_ref, strict_ref, incl_ref, eye_ref,
                 o_ref,
                 s_ref, y_ref, ak_ref, abr_ref, t_ref):
    i = pl.program_id(0)
    nb, tb, rw = o_ref.shape
    npair = rw // PAIR
    cpb = tb // CHUNK

    @pl.when(i == 0)
    def _():
        s_ref[...] = jnp.zeros_like(s_ref)

    strict = strict_ref[...]
    incl = incl_ref[...]
    eye = eye_ref[...]
    m0 = hmask_ref[0]
    m1 = hmask_ref[1]
    pairs = range(npair)
    lanes = [slice(pi * PAIR, (pi + 1) * PAIR) for pi in pairs]
    stack = lambda val: _stack(val, m0, m1)

    def rows_of(j):
        return pl.ds(pl.multiple_of(j * CHUNK, CHUNK), CHUNK)

    def phase_a(gi, carry):
        inst = [(bi, gi * A_GROUP + j, pi) for bi in range(nb) for j in range(A_GROUP) for pi in pairs]
        sl = [(bi, rows_of(j), lanes[pi]) for bi, j, pi in inst]
        lhs = [jnp.concatenate([kt_ref[x], rt_ref[x]], axis=0) for x in sl]
        kb = [jnp.concatenate([stack(kh_ref[x]), stack(bh_ref[x])], axis=0) for x in sl]
        g = [_dg(l_, k_, NT) for l_, k_ in zip(lhs, kb)]
        for g_, (bi, j, pi) in zip(g, inst):
            ci = bi * cpb + j
            ak_ref[ci, pi, 0:CHUNK] = (g_[0:CHUNK, 0:PAIR] * strict).astype(BF16)
            ak_ref[ci, pi, CHUNK:] = (g_[CHUNK:, 0:PAIR] * incl).astype(BF16)
            abr_ref[ci, pi] = (g_[CHUNK:, PAIR:] * incl).astype(BF16)
        t_w = _neumann_inverse([-(g_[0:CHUNK, PAIR:] * strict) for g_ in g], eye, m0, m1)
        for t_, (bi, j, pi) in zip(t_w, inst):
            t_ref[bi * cpb + j, pi] = t_.astype(BF16)
        return carry

    lax.fori_loop(0, cpb // A_GROUP, phase_a, 0)

    def phase_b(j, carry):
        chains = [(bi, pi) for bi in range(nb) for pi in pairs]
        cis = [bi * cpb + j for bi, _ in chains]
        rs = rows_of(j)
        sl = [(bi, rs, lanes[pi]) for bi, pi in chains]
        lhs = [jnp.concatenate([kt_ref[x], rt_ref[x]], axis=0) for x in sl]
        kb = [jnp.concatenate([stack(kh_ref[x]), -stack(bh_ref[x])], axis=0) for x in sl]
        vs = [stack(v_ref[x]) for x in sl]
        s = [s_ref[bi * npair + pi] for bi, pi in chains]
        sb = [x.astype(BF16) for x in s]
        xy = [_dg(l_, sb_, NT) + _dot(ak_ref[ci, pi], v_)
              for l_, sb_, v_, ci, (_, pi) in zip(lhs, sb, vs, cis, chains)]
        us = [stack(_dot(t_ref[ci, pi], stack(x_[0:CHUNK].astype(BF16))).astype(BF16))
              for x_, ci, (_, pi) in zip(xy, cis, chains)]
        ds_ = [_dg(jnp.concatenate([v_, u_], axis=0), k_, TN) for v_, u_, k_ in zip(vs, us, kb)]
        for s_, d_, (bi, pi) in zip(s, ds_, chains):
            e_l = el_ref[bi, j][0:1, lanes[pi]]
            s_ref[bi * npair + pi] = (s_ + d_) * e_l
        for x_, u_, x, ci, (_, pi) in zip(xy, us, sl, cis, chains):
            y_ref[x] = x_[CHUNK:] - _dot(abr_ref[ci, pi], u_)
        return carry

    lax.fori_loop(0, cpb, phase_b, 0, unroll=2)

    bd = bd_ref[...]
    y = y_ref[...].reshape(nb * tb, rw)
    inv = 1.0 / HEAD
    mean = _dot(y.astype(BF16), bd) * inv
    d = y - mean
    var = _dot((d * d).astype(BF16), bd) * inv
    yl = d * lax.rsqrt(var + LNX_EPS) * lg_ref[...] + lb_ref[...]
    bonus = bonus_ref[...].reshape(nb * tb, rw).astype(F32)
    gate = gate_ref[...].reshape(nb * tb, rw).astype(F32)
    o_ref[...] = ((yl + bonus) * gate).astype(o_ref.dtype).reshape(nb, tb, rw)


def _rwkv_call(streams, el, consts, tb):
    b, t, rw = streams[0].shape
    npair = rw // PAIR
    names = ["lnx_g", "lnx_b", "bd", "hmask", "strict", "incl", "eye"]
    ops = [consts[nm] for nm in names]
    full = lambda arr: pl.BlockSpec(arr.shape, lambda i: (0,) * arr.ndim)
    tok = pl.BlockSpec((b, tb, rw), lambda i: (0, i, 0))
    return pl.pallas_call(
        _rwkv_kernel,
        grid=(t // tb,),
        in_specs=[tok] * len(streams)
                 + [pl.BlockSpec((b, tb // CHUNK, SUBLANES, rw), lambda i: (0, i, 0, 0))]
                 + [full(a) for a in ops],
        out_specs=tok,
        out_shape=jax.ShapeDtypeStruct((b, t, rw), BF16),
        scratch_shapes=[pltpu.VMEM((b * npair, PAIR, PAIR), F32), pltpu.VMEM((b, tb, rw), F32)]
                       + [pltpu.VMEM((b * tb // CHUNK, npair, 2 * CHUNK, PAIR), BF16)]
                       + [pltpu.VMEM((b * tb // CHUNK, npair, CHUNK, PAIR), BF16)] * 2,
        compiler_params=pltpu.CompilerParams(dimension_semantics=("arbitrary",),
                                             vmem_limit_bytes=VMEM_LIMIT),
        name="rwkv_group",
    )(*streams, el, *ops)


def _ffn_kernel(x_ref, h_ref, mod_ref, gpost_ref, wup_ref, cw_ref, cb_ref, wdn_ref,
                o_ref, z_ref, zc_ref, acc_ref):
    i = pl.program_id(1)
    tm = x_ref.shape[1]
    d_ff = wdn_ref.shape[0]
    hb = h_ref[0]

    @pl.when(i == 0)
    def _():
        zc_ref[...] = jnp.zeros_like(zc_ref)

    n_chunks = d_ff // FF_CHUNK
    cols = lambda f: (slice(f * FF_CHUNK, (f + 1) * FF_CHUNK),
                      slice(d_ff + f * FF_CHUNK, d_ff + (f + 1) * FF_CHUNK))

    def up(f):
        cg, cv = cols(f)
        return jnp.concatenate([_dot(hb, wup_ref[:, cg]), _dot(hb, wup_ref[:, cv])], axis=1)

    def conv_act(f, z):
        cg, cv = cols(f)
        zb = z_ref.at[f % 2]
        zb[0:SUBLANES, :] = zc_ref[f]
        zb[SUBLANES:, :] = z
        zc_ref[f] = z[tm - SUBLANES:, :]
        zc = jnp.concatenate([cb_ref[:, cg], cb_ref[:, cv]], axis=1)
        for k in range(FFN_K):
            o = SUBLANES - (FFN_K - 1) + k
            wk = jnp.concatenate([cw_ref[k:k + 1, cg], cw_ref[k:k + 1, cv]], axis=1)
            zc = zc + wk * zb[o:o + tm, :]
        zg = zc[:, :FF_CHUNK]
        return (zg * _sigmoid(zg) * zc[:, FF_CHUNK:]).astype(BF16)

    act = conv_act(0, up(0))
    z_next = up(1)
    for f in range(n_chunks):
        z_cur = z_next
        if f + 2 < n_chunks:
            z_next = up(f + 2)
        contrib = _dot(act, wdn_ref[f * FF_CHUNK:(f + 1) * FF_CHUNK, :])
        if f == 0:
            acc_ref[...] = contrib
        else:
            acc_ref[...] += contrib
        if f + 1 < n_chunks:
            act = conv_act(f + 1, z_cur)
    o_ref[0] = x_ref[0] + mod_ref[0, 5:6, :] * _rms(acc_ref[...], gpost_ref[...])


def _ffn_call(x1, h2, mod, gpost, wup, cw, cb, wdn, tm):
    b, t, d = x1.shape
    full = lambda arr: pl.BlockSpec(arr.shape, lambda bi, i: (0,) * arr.ndim)
    return pl.pallas_call(
        _ffn_kernel,
        grid=(b, t // tm),
        in_specs=[pl.BlockSpec((1, tm, d), lambda bi, i: (bi, i, 0)),
                  pl.BlockSpec((1, tm, d), lambda bi, i: (bi, i, 0)),
                  pl.BlockSpec((1, 6, d), lambda bi, i: (bi, 0, 0)),
                  full(gpost), full(wup), full(cw), full(cb), full(wdn)],
        out_specs=pl.BlockSpec((1, tm, d), lambda bi, i: (bi, i, 0)),
        out_shape=jax.ShapeDtypeStruct((b, t, d), F32),
        scratch_shapes=[pltpu.VMEM((2, tm + SUBLANES, 2 * FF_CHUNK), F32),
                        pltpu.VMEM((wdn.shape[0] // FF_CHUNK, SUBLANES, 2 * FF_CHUNK), F32),
                        pltpu.VMEM((tm, d), F32)],
        compiler_params=pltpu.CompilerParams(dimension_semantics=("arbitrary", "arbitrary"),
                                             vmem_limit_bytes=VMEM_LIMIT),
        name="conv_ffn",
    )(x1, h2, mod, gpost, wup, cw, cb, wdn)


def _pad_cols(w, n):
    return jnp.pad(w, ((0, 0), (0, n - w.shape[1])))


def _pad_rows(w, n):
    return jnp.pad(w, ((0, n - w.shape[0]), (0, 0)))


def _layer(x, mod, lw, tiles):
    b, t, d = x.shape
    cw = lw["conv_dw_w"].shape[1]
    rw = lw["w0"].shape[0]
    dl = lw["w2"].shape[0]
    al = lw["a2"].shape[0]
    row = lambda vec: vec.reshape(1, -1)

    o = 2 * cw + 3 * rw
    w_in = lw["w_in"].astype(BF16)
    w_in_r = jnp.concatenate(
        [w_in[:, :o], _pad_cols(w_in[:, o:o + dl], 128), _pad_cols(w_in[:, o + dl:o + dl + al], 128),
         _pad_cols(w_in[:, o + dl + al:], 256)], axis=1)
    mu = row(lw["rwkv_mu"])
    o2 = 3 * rw
    mu_r = jnp.concatenate([mu[:, :o2], _pad_cols(mu[:, o2:o2 + dl], 128),
                            _pad_cols(mu[:, o2 + dl:o2 + dl + al], 128),
                            _pad_cols(mu[:, o2 + dl + al:], 256)], axis=1)

    ch = jnp.arange(rw) // HEAD
    rr = jnp.arange(SUB)
    pp = jnp.arange(PAIR)
    tt = jnp.arange(CHUNK)
    first = (pp < HEAD).astype(BF16)
    consts = {
        "mu": mu_r,
        "w0": row(lw["w0"]), "w2": _pad_rows(lw["w2"], 128).astype(BF16),
        "a0": row(lw["a0"]), "a2": _pad_rows(lw["a2"], 128).astype(BF16),
        "g2": _pad_rows(lw["g2"], 256).astype(BF16),
        "k_k": row(lw["k_k"]), "k_a": row(lw["k_a"]), "r_k": row(lw["r_k"]),
        "lnx_g": row(lw["lnx_g"]), "lnx_b": row(lw["lnx_b"]),
        "bd": (ch[:, None] == ch[None, :]).astype(BF16),
        "tri": ((rr[:, None] // CHUNK == rr[None, :] // CHUNK) & (rr[None, :] <= rr[:, None])).astype(BF16),
        "hmask": jnp.broadcast_to(jnp.stack([first, 1 - first])[:, None, :], (2, CHUNK, PAIR)),
        "strict": ((pp[None, :] % CHUNK) < tt[:, None]).astype(F32),
        "incl": ((pp[None, :] % CHUNK) <= tt[:, None]).astype(F32),
        "eye": ((pp[None, :] % CHUNK) == tt[:, None]).astype(F32),
    }

    pc, *streams, el = _front_call(x, mod, row(lw["mix_pre_g"]), w_in_r, consts, 2 * cw, tiles["inproj"])
    conv_w = jnp.broadcast_to(lw["conv_dw_w"][:, None, :], (CONV_K, SUBLANES, cw))
    y = _rwkv_call(streams, el, consts, tiles["rwkv"])
    x1, h2 = _mixout_call(pc, conv_w, row(lw["conv_dw_b"]), row(lw["conv_ln_g"]), row(lw["conv_ln_b"]), y, x,
                          mod, row(lw["mix_post_g"]), row(lw["ffn_pre_g"]), lw["w_out"].astype(BF16),
                          tiles["mixout"])
    return _ffn_call(x1, h2, mod, row(lw["ffn_post_g"]), lw["w_up"].astype(BF16), lw["ffn_dw_w"],
                     row(lw["ffn_dw_b"]), lw["w_down"].astype(BF16), tiles["ffn"])


def _tiles(t):
    pick = lambda pref: min(pref, t)
    return {"inproj": pick(512), "rwkv": pick(512), "mixout": pick(256), "ffn": pick(256)}


def kernel(x, c, ada_w, ada_b, mix_pre_g, mix_post_g, w_in, conv_dw_w, conv_dw_b, conv_ln_g,
           conv_ln_b, rwkv_mu, w0, w2, a0, a2, g2, k_k, k_a, r_k, lnx_g, lnx_b, w_out, ffn_pre_g,
           ffn_post_g, w_up, ffn_dw_w, ffn_dw_b, w_down):
    b, t, d = x.shape
    depth = ada_w.shape[0]
    tiles = _tiles(t)
    c8 = jnp.pad(c, ((0, 8 - b), (0, 0)))
    for l in range(depth):
        mod = _mod_call(c8, ada_w[l], ada_b[l].reshape(1, -1), 1536)[:b].reshape(b, 6, d)
        lw = {"mix_pre_g": mix_pre_g[l], "mix_post_g": mix_post_g[l], "w_in": w_in[l],
              "conv_dw_w": conv_dw_w[l], "conv_dw_b": conv_dw_b[l], "conv_ln_g": conv_ln_g[l],
              "conv_ln_b": conv_ln_b[l], "rwkv_mu": rwkv_mu[l], "w0": w0[l], "w2": w2[l],
              "a0": a0[l], "a2": a2[l], "g2": g2[l], "k_k": k_k[l], "k_a": k_a[l],
              "r_k": r_k[l].reshape(-1), "lnx_g": lnx_g[l], "lnx_b": lnx_b[l], "w_out": w_out[l],
              "ffn_pre_g": ffn_pre_g[l], "ffn_post_g": ffn_post_g[l], "w_up": w_up[l],
              "ffn_dw_w": ffn_dw_w[l], "ffn_dw_b": ffn_dw_b[l], "w_down": w_down[l]}
        x = _layer(x, mod, lw, tiles)
    return x
```

```python
import functools
import math

import jax
import jax.numpy as jnp
from jax import lax
from jax.experimental import pallas as pl
from jax.experimental.pallas import tpu as pltpu

F32 = jnp.float32
BF16 = jnp.bfloat16

HEAD = 64
CHUNK = 64
PAIR = 2 * HEAD
CONV_K = 31
FFN_K = 3
RMS_EPS = 1e-6
LN_EPS = 1e-5
LNX_EPS = 64e-5
SUBLANES = 8
CONV_HALO = 32
FF_CHUNK = 256
A_GROUP = 4
SUB = 256
VMEM_LIMIT = 56 * 1024 * 1024

NT = (((1,), (1,)), ((), ()))
TN = (((0,), (0,)), ((), ()))


def _dot(a, b):
    return jnp.dot(a, b, preferred_element_type=F32)


def _dg(a, b, dims):
    return lax.dot_general(a, b, dims, preferred_element_type=F32)


def _sigmoid(x):
    return jax.nn.sigmoid(x)


def _rms(x, g):
    ms = jnp.mean(x * x, axis=-1, keepdims=True)
    return x * lax.rsqrt(ms + RMS_EPS) * g


def _mod_kernel(c_ref, w_ref, b_ref, o_ref):
    cs = c_ref[...]
    s = cs * _sigmoid(cs)
    o_ref[...] = jnp.dot(s, w_ref[...], preferred_element_type=F32,
                         precision=lax.Precision.HIGHEST) + b_ref[...]


def _mod_call(c8, ada_w, ada_b, tn):
    d, n = ada_w.shape
    return pl.pallas_call(
        _mod_kernel,
        grid=(n // tn,),
        in_specs=[pl.BlockSpec((8, d), lambda j: (0, 0)),
                  pl.BlockSpec((d, tn), lambda j: (0, j)),
                  pl.BlockSpec((1, tn), lambda j: (0, j))],
        out_specs=pl.BlockSpec((8, tn), lambda j: (0, j)),
        out_shape=jax.ShapeDtypeStruct((8, n), F32),
        compiler_params=pltpu.CompilerParams(dimension_semantics=("arbitrary",),
                                             vmem_limit_bytes=VMEM_LIMIT),
        name="adaln_mod",
    )(c8, ada_w, ada_b)


def _front_kernel(x_ref, mod_ref, g_ref, w_ref, mu_ref, w0_ref, w2_ref, a0_ref, a2_ref, g2_ref,
                  kk_ref, ka_ref, rk_ref, bd_ref, tri_ref,
                  u0_ref, kt_ref, rt_ref, kh_ref, bh_ref, v_ref, bonus_ref, gate_ref, el_ref,
                  prev_ref):
    i = pl.program_id(1)
    tm = x_ref.shape[1]
    cw = u0_ref.shape[2]
    nc = 2 * cw
    rw = v_ref.shape[2]

    @pl.when(i == 0)
    def _():
        prev_ref[...] = jnp.zeros_like(prev_ref)

    bd = bd_ref[...]
    tri = tri_ref[...]
    rows = lax.broadcasted_iota(jnp.int32, (SUB, 1), 0)

    def prep(p, r0):
        prev = jnp.where(rows == 0, prev_ref[0:1, :], pltpu.roll(p, 1, axis=0))
        prev_ref[0:1, :] = p[SUB - 1:SUB, :]
        xs = p + (prev - p) * mu_ref[...]
        r = xs[:, 0:rw]
        k = xs[:, rw:2 * rw]
        v = xs[:, 2 * rw:3 * rw]
        o = 3 * rw
        wd = xs[:, o:o + 128]
        ad = xs[:, o + 128:o + 256]
        gd = xs[:, o + 256:o + 512]
        zw = w0_ref[...] + _dot(jnp.tanh(wd).astype(BF16), w2_ref[...])
        lw = -math.exp(-0.5) * _sigmoid(zw)
        a = _sigmoid(a0_ref[...] + _dot(ad.astype(BF16), a2_ref[...]))
        gate = _dot(_sigmoid(gd).astype(BF16), g2_ref[...])
        kkr = k * kk_ref[...]
        ss = _dot((kkr * kkr).astype(BF16), bd)
        kk = kkr * lax.rsqrt(jnp.maximum(ss, 1e-24))
        km = k * (1.0 + (a - 1.0) * ka_ref[...])
        bonus = _dot((r * km * rk_ref[...]).astype(BF16), bd) * v
        hi = lw.astype(BF16)
        r1 = lw - hi.astype(F32)
        mid = r1.astype(BF16)
        lo = (r1 - mid.astype(F32)).astype(BF16)
        c = _dot(tri, hi) + _dot(tri, mid) + _dot(tri, lo)
        ec = jnp.exp(c)
        ecn = jnp.exp(-c)
        rs = slice(r0, r0 + SUB)
        kt_ref[0, rs, :] = (kk * jnp.exp(c - lw)).astype(BF16)
        rt_ref[0, rs, :] = (r * ec).astype(BF16)
        kh_ref[0, rs, :] = (km * ecn).astype(BF16)
        bh_ref[0, rs, :] = (a * kk * ecn).astype(BF16)
        v_ref[0, rs, :] = v.astype(BF16)
        bonus_ref[0, rs, :] = bonus.astype(BF16)
        gate_ref[0, rs, :] = gate.astype(BF16)
        for j in range(SUB // CHUNK):
            last = ec[(j + 1) * CHUNK - 1:(j + 1) * CHUNK, :]
            el_ref[0, r0 // CHUNK + j] = jnp.broadcast_to(last, (SUBLANES, rw))

    def proj(r0):
        x = x_ref[0, r0:r0 + SUB, :]
        h = _rms(x, g_ref[...]) * (1.0 + mod_ref[0, 1:2, :]) + mod_ref[0, 0:1, :]
        return _dot(h.astype(BF16), w_ref[...])

    p_next = proj(0)
    for r0 in range(0, tm, SUB):
        p = p_next
        if r0 + SUB < tm:
            p_next = proj(r0 + SUB)
        u0_ref[0, r0:r0 + SUB, :] = (p[:, :cw] * _sigmoid(p[:, cw:nc])).astype(BF16)
        prep(p[:, nc:], r0)


def _front_call(x, mod, g, w, consts, n_conv, tm):
    b, t, d = x.shape
    n = w.shape[1]
    rw = consts["w0"].shape[1]
    names = ["mu", "w0", "w2", "a0", "a2", "g2", "k_k", "k_a", "r_k", "bd", "tri"]
    ops = [consts[nm] for nm in names]
    full = lambda arr: pl.BlockSpec(arr.shape, lambda bi, i: (0,) * arr.ndim)
    tok = lambda width: pl.BlockSpec((1, tm, width), lambda bi, i: (bi, i, 0))
    tok_shape = lambda width, dt: jax.ShapeDtypeStruct((b, t, width), dt)
    return pl.pallas_call(
        _front_kernel,
        grid=(b, t // tm),
        in_specs=[tok(d), pl.BlockSpec((1, 6, d), lambda bi, i: (bi, 0, 0)),
                  pl.BlockSpec((1, d), lambda bi, i: (0, 0)),
                  pl.BlockSpec((d, n), lambda bi, i: (0, 0))] + [full(a) for a in ops],
        out_specs=[tok(n_conv // 2)] + [tok(rw)] * 7
                  + [pl.BlockSpec((1, tm // CHUNK, SUBLANES, rw), lambda bi, i: (bi, i, 0, 0))],
        out_shape=[tok_shape(n_conv // 2, BF16)] + [tok_shape(rw, BF16)] * 7
                  + [jax.ShapeDtypeStruct((b, t // CHUNK, SUBLANES, rw), F32)],
        scratch_shapes=[pltpu.VMEM((SUBLANES, n - n_conv), F32)],
        compiler_params=pltpu.CompilerParams(dimension_semantics=("arbitrary", "arbitrary"),
                                             vmem_limit_bytes=VMEM_LIMIT),
        name="inproj_prep",
    )(x, mod, g, w, *ops)


def _mixout_kernel(u0_ref, halo_ref, cwt_ref, cb_ref, lg_ref, lb_ref, y_ref, x_ref, mod_ref, g_ref, gf_ref,
                   w_ref, o_ref, h_ref, scr, shifted, u_scr, my_scr, *, rb):
    i = pl.program_id(1)
    tm = u0_ref.shape[1]
    cw = y_ref.shape[2]
    my_scr[...] = _dot(y_ref[0], w_ref[cw:, :])

    scr[CONV_HALO:CONV_HALO + tm, :] = u0_ref[0].astype(F32)
    scr[0:CONV_HALO, :] = jnp.where(i > 0, halo_ref[0].astype(F32), 0.0)
    span = tm + CONV_HALO - SUBLANES
    for r in range(1, SUBLANES):
        shifted[r - 1, 0:span, :] = scr[r:r + span, :]
    off = CONV_HALO - (CONV_K - 1)
    for r0 in range(0, tm, rb):
        acc = jnp.zeros((rb // SUBLANES, SUBLANES, cw), F32) + cb_ref[...]
        for k in range(CONV_K):
            r = (off + k) % SUBLANES
            a0 = r0 + off + k - r
            src = scr[a0:a0 + rb, :] if r == 0 else shifted[r - 1, a0:a0 + rb, :]
            acc = acc + cwt_ref[k] * src.reshape(rb // SUBLANES, SUBLANES, cw)
        acc = acc.reshape(rb, cw)
        mu = jnp.mean(acc, axis=-1, keepdims=True)
        dd = acc - mu
        var = jnp.mean(dd * dd, axis=-1, keepdims=True)
        z = dd * lax.rsqrt(var + LN_EPS) * lg_ref[...] + lb_ref[...]
        u_scr[r0:r0 + rb, :] = (z * _sigmoid(z)).astype(BF16)

    m = _dot(u_scr[...], w_ref[0:cw, :]) + my_scr[...]
    x1 = x_ref[0] + mod_ref[0, 2:3, :] * _rms(m, g_ref[...])
    o_ref[0] = x1
    h_ref[0] = (_rms(x1, gf_ref[...]) * (1.0 + mod_ref[0, 4:5, :]) + mod_ref[0, 3:4, :]).astype(h_ref.dtype)


def _mixout_call(u0, conv_w, conv_b, lg, lb, y, x, mod, g, gf, w, tm, rb=32):
    b, t, d = x.shape
    cw = u0.shape[2]
    hb = tm // CONV_HALO
    row_blk = lambda width: pl.BlockSpec((1, tm, width), lambda bi, i: (bi, i, 0))
    vec = lambda width: pl.BlockSpec((1, width), lambda bi, i: (0, 0))
    return pl.pallas_call(
        functools.partial(_mixout_kernel, rb=rb),
        grid=(b, t // tm),
        in_specs=[row_blk(cw),
                  pl.BlockSpec((1, CONV_HALO, cw), lambda bi, i: (bi, jnp.maximum(i * hb - 1, 0), 0)),
                  pl.BlockSpec((CONV_K, SUBLANES, cw), lambda bi, i: (0, 0, 0)),
                  vec(cw), vec(cw), vec(cw),
                  row_blk(y.shape[2]), row_blk(d),
                  pl.BlockSpec((1, 6, d), lambda bi, i: (bi, 0, 0)),
                  vec(d), vec(d),
                  pl.BlockSpec(w.shape, lambda bi, i: (0, 0))],
        out_specs=[row_blk(d), row_blk(d)],
        out_shape=[jax.ShapeDtypeStruct((b, t, d), F32), jax.ShapeDtypeStruct((b, t, d), BF16)],
        scratch_shapes=[pltpu.VMEM((tm + CONV_HALO, cw), F32),
                        pltpu.VMEM((SUBLANES - 1, tm + CONV_HALO - SUBLANES, cw), F32),
                        pltpu.VMEM((tm, cw), BF16),
                        pltpu.VMEM((tm, d), F32)],
        compiler_params=pltpu.CompilerParams(dimension_semantics=("arbitrary", "arbitrary"),
                                             vmem_limit_bytes=VMEM_LIMIT),
        name="conv_outproj",
    )(u0, u0, conv_w, conv_b, lg, lb, y, x, mod, g, gf, w)


def _stack(val, m0, m1):
    return jnp.concatenate([val * m0, val * m1], axis=0)


def _neumann_inverse(n_w, eye_w, m0, m1):
    levels = CHUNK.bit_length() - 2
    t_w = [eye_w + n_ for n_ in n_w]
    pb = [n_.astype(BF16) for n_ in n_w]
    pk = [_dot(x, _stack(x, m0, m1)) for x in pb]
    for k in range(1, levels + 1):
        pkb = [x.astype(BF16) for x in pk]
        pd = [_stack(x, m0, m1) for x in pkb]
        if k < levels:
            both = [_dot(jnp.concatenate([t_.astype(BF16), p_], axis=0), d_)
                    for t_, p_, d_ in zip(t_w, pkb, pd)]
            t_w = [t_ + b_[:CHUNK] for t_, b_ in zip(t_w, both)]
            pk = [b_[CHUNK:] for b_ in both]
        else:
            t_w = [t_ + _dot(t_.astype(BF16), d_) for t_, d_ in zip(t_w, pd)]
    return t_w


def _rwkv_kernel(kt_ref, rt_ref, kh_ref, bh_ref, v_ref, bonus_ref, gate_ref, el_ref,
                 lg_ref, lb_ref, bd_ref, hmask_ref, strict_ref, incl_ref, eye_ref,
                 o_ref,
                 s_ref, y_ref, ak_ref, abr_ref, t_ref):
    i = pl.program_id(0)
    nb, tb, rw = o_ref.shape
    npair = rw // PAIR
    cpb = tb // CHUNK

    @pl.when(i == 0)
    def _():
        s_ref[...] = jnp.zeros_like(s_ref)

    strict = strict_ref[...]
    incl = incl_ref[...]
    eye = eye_ref[...]
    m0 = hmask_ref[0]
    m1 = hmask_ref[1]
    pairs = range(npair)
    lanes = [slice(pi * PAIR, (pi + 1) * PAIR) for pi in pairs]
    stack = lambda val: _stack(val, m0, m1)

    def rows_of(j):
        return pl.ds(pl.multiple_of(j * CHUNK, CHUNK), CHUNK)

    def phase_a(gi, carry):
        inst = [(bi, gi * A_GROUP + j, pi) for bi in range(nb) for j in range(A_GROUP) for pi in pairs]
        sl = [(bi, rows_of(j), lanes[pi]) for bi, j, pi in inst]
        lhs = [jnp.concatenate([kt_ref[x], rt_ref[x]], axis=0) for x in sl]
        kb = [jnp.concatenate([stack(kh_ref[x]), stack(bh_ref[x])], axis=0) for x in sl]
        g = [_dg(l_, k_, NT) for l_, k_ in zip(lhs, kb)]
        for g_, (bi, j, pi) in zip(g, inst):
            ci = bi * cpb + j
            ak_ref[ci, pi, 0:CHUNK] = (g_[0:CHUNK, 0:PAIR] * strict).astype(BF16)
            ak_ref[ci, pi, CHUNK:] = (g_[CHUNK:, 0:PAIR] * incl).astype(BF16)
            abr_ref[ci, pi] = (g_[CHUNK:, PAIR:] * incl).astype(BF16)
        t_w = _neumann_inverse([-(g_[0:CHUNK, PAIR:] * strict) for g_ in g], eye, m0, m1)
        for t_, (bi, j, pi) in zip(t_w, inst):
            t_ref[bi * cpb + j, pi] = t_.astype(BF16)
        return carry

    lax.fori_loop(0, cpb // A_GROUP, phase_a, 0)

    def phase_b(j, carry):
        chains = [(bi, pi) for bi in range(nb) for pi in pairs]
        cis = [bi * cpb + j for bi, _ in chains]
        rs = rows_of(j)
        sl = [(bi, rs, lanes[pi]) for bi, pi in chains]
        lhs = [jnp.concatenate([kt_ref[x], rt_ref[x]], axis=0) for x in sl]
        kb = [jnp.concatenate([stack(kh_ref[x]), -stack(bh_ref[x])], axis=0) for x in sl]
        vs = [stack(v_ref[x]) for x in sl]
        s = [s_ref[bi * npair + pi] for bi, pi in chains]
        sb = [x.astype(BF16) for x in s]
        xy = [_dg(l_, sb_, NT) + _dot(ak_ref[ci, pi], v_)
              for l_, sb_, v_, ci, (_, pi) in zip(lhs, sb, vs, cis, chains)]
        us = [stack(_dot(t_ref[ci, pi], stack(x_[0:CHUNK].astype(BF16))).astype(BF16))
              for x_, ci, (_, pi) in zip(xy, cis, chains)]
        ds_ = [_dg(jnp.concatenate([v_, u_], axis=0), k_, TN) for v_, u_, k_ in zip(vs, us, kb)]
        for s_, d_, (bi, pi) in zip(s, ds_, chains):
            e_l = el_ref[bi, j][0:1, lanes[pi]]
            s_ref[bi * npair + pi] = (s_ + d_) * e_l
        for x_, u_, x, ci, (_, pi) in zip(xy, us, sl, cis, chains):
            y_ref[x] = x_[CHUNK:] - _dot(abr_ref[ci, pi], u_)
        return carry

    lax.fori_loop(0, cpb, phase_b, 0, unroll=4)

    bd = bd_ref[...]
    y = y_ref[...].reshape(nb * tb, rw)
    inv = 1.0 / HEAD
    mean = _dot(y.astype(BF16), bd) * inv
    d = y - mean
    var = _dot((d * d).astype(BF16), bd) * inv
    yl = d * lax.rsqrt(var + LNX_EPS) * lg_ref[...] + lb_ref[...]
    bonus = bonus_ref[...].reshape(nb * tb, rw).astype(F32)
    gate = gate_ref[...].reshape(nb * tb, rw).astype(F32)
    o_ref[...] = ((yl + bonus) * gate).astype(o_ref.dtype).reshape(nb, tb, rw)


def _rwkv_call(streams, el, consts, tb):
    b, t, rw = streams[0].shape
    npair = rw // PAIR
    names = ["lnx_g", "lnx_b", "bd", "hmask", "strict", "incl", "eye"]
    ops = [consts[nm] for nm in names]
    full = lambda arr: pl.BlockSpec(arr.shape, lambda i: (0,) * arr.ndim)
    tok = pl.BlockSpec((b, tb, rw), lambda i: (0, i, 0))
    return pl.pallas_call(
        _rwkv_kernel,
        grid=(t // tb,),
        in_specs=[tok] * len(streams)
                 + [pl.BlockSpec((b, tb // CHUNK, SUBLANES, rw), lambda i: (0, i, 0, 0))]
                 + [full(a) for a in ops],
        out_specs=tok,
        out_shape=jax.ShapeDtypeStruct((b, t, rw), BF16),
        scratch_shapes=[pltpu.VMEM((b * npair, PAIR, PAIR), F32), pltpu.VMEM((b, tb, rw), F32)]
                       + [pltpu.VMEM((b * tb // CHUNK, npair, 2 * CHUNK, PAIR), BF16)]
                       + [pltpu.VMEM((b * tb // CHUNK, npair, CHUNK, PAIR), BF16)] * 2,
        compiler_params=pltpu.CompilerParams(dimension_semantics=("arbitrary",),
                                             vmem_limit_bytes=VMEM_LIMIT),
        name="rwkv_group",
    )(*streams, el, *ops)


def _ffn_kernel(x_ref, h_ref, mod_ref, gpost_ref, wup_ref, cw_ref, cb_ref, wdn_ref,
                o_ref, z_ref, zc_ref, acc_ref):
    i = pl.program_id(1)
    tm = x_ref.shape[1]
    d_ff = wdn_ref.shape[0]
    hb = h_ref[0]

    @pl.when(i == 0)
    def _():
        zc_ref[...] = jnp.zeros_like(zc_ref)

    n_chunks = d_ff // FF_CHUNK
    cols = lambda f: (slice(f * FF_CHUNK, (f + 1) * FF_CHUNK),
                      slice(d_ff + f * FF_CHUNK, d_ff + (f + 1) * FF_CHUNK))

    def up(f):
        cg, cv = cols(f)
        return jnp.concatenate([_dot(hb, wup_ref[:, cg]), _dot(hb, wup_ref[:, cv])], axis=1)

    def conv_act(f, z):
        cg, cv = cols(f)
        zb = z_ref.at[f % 2]
        zb[0:SUBLANES, :] = zc_ref[f]
        zb[SUBLANES:, :] = z
        zc_ref[f] = z[tm - SUBLANES:, :]
        zc = jnp.concatenate([cb_ref[:, cg], cb_ref[:, cv]], axis=1)
        for k in range(FFN_K):
            o = SUBLANES - (FFN_K - 1) + k
            wk = jnp.concatenate([cw_ref[k:k + 1, cg], cw_ref[k:k + 1, cv]], axis=1)
            zc = zc + wk * zb[o:o + tm, :]
        zg = zc[:, :FF_CHUNK]
        return (zg * _sigmoid(zg) * zc[:, FF_CHUNK:]).astype(BF16)

    act = conv_act(0, up(0))
    z_next = up(1)
    for f in range(n_chunks):
        z_cur = z_next
        if f + 2 < n_chunks:
            z_next = up(f + 2)
        contrib = _dot(act, wdn_ref[f * FF_CHUNK:(f + 1) * FF_CHUNK, :])
        if f == 0:
            acc_ref[...] = contrib
        else:
            acc_ref[...] += contrib
        if f + 1 < n_chunks:
            act = conv_act(f + 1, z_cur)
    o_ref[0] = x_ref[0] + mod_ref[0, 5:6, :] * _rms(acc_ref[...], gpost_ref[...])


def _ffn_call(x1, h2, mod, gpost, wup, cw, cb, wdn, tm):
    b, t, d = x1.shape
    full = lambda arr: pl.BlockSpec(arr.shape, lambda bi, i: (0,) * arr.ndim)
    return pl.pallas_call(
        _ffn_kernel,
        grid=(b, t // tm),
        in_specs=[pl.BlockSpec((1, tm, d), lambda bi, i: (bi, i, 0)),
                  pl.BlockSpec((1, tm, d), lambda bi, i: (bi, i, 0)),
                  pl.BlockSpec((1, 6, d), lambda bi, i: (bi, 0, 0)),
                  full(gpost), full(wup), full(cw), full(cb), full(wdn)],
        out_specs=pl.BlockSpec((1, tm, d), lambda bi, i: (bi, i, 0)),
        out_shape=jax.ShapeDtypeStruct((b, t, d), F32),
        scratch_shapes=[pltpu.VMEM((2, tm + SUBLANES, 2 * FF_CHUNK), F32),
                        pltpu.VMEM((wdn.shape[0] // FF_CHUNK, SUBLANES, 2 * FF_CHUNK), F32),
                        pltpu.VMEM((tm, d), F32)],
        compiler_params=pltpu.CompilerParams(dimension_semantics=("arbitrary", "arbitrary"),
                                             vmem_limit_bytes=VMEM_LIMIT),
        name="conv_ffn",
    )(x1, h2, mod, gpost, wup, cw, cb, wdn)


def _pad_cols(w, n):
    return jnp.pad(w, ((0, 0), (0, n - w.shape[1])))


def _pad_rows(w, n):
    return jnp.pad(w, ((0, n - w.shape[0]), (0, 0)))


def _layer(x, mod, lw, tiles):
    b, t, d = x.shape
    cw = lw["conv_dw_w"].shape[1]
    rw = lw["w0"].shape[0]
    dl = lw["w2"].shape[0]
    al = lw["a2"].shape[0]
    row = lambda vec: vec.reshape(1, -1)

    o = 2 * cw + 3 * rw
    w_in = lw["w_in"].astype(BF16)
    w_in_r = jnp.concatenate(
        [w_in[:, :o], _pad_cols(w_in[:, o:o + dl], 128), _pad_cols(w_in[:, o + dl:o + dl + al], 128),
         _pad_cols(w_in[:, o + dl + al:], 256)], axis=1)
    mu = row(lw["rwkv_mu"])
    o2 = 3 * rw
    mu_r = jnp.concatenate([mu[:, :o2], _pad_cols(mu[:, o2:o2 + dl], 128),
                            _pad_cols(mu[:, o2 + dl:o2 + dl + al], 128),
                            _pad_cols(mu[:, o2 + dl + al:], 256)], axis=1)

    ch = jnp.arange(rw) // HEAD
    rr = jnp.arange(SUB)
    pp = jnp.arange(PAIR)
    tt = jnp.arange(CHUNK)
    first = (pp < HEAD).astype(BF16)
    consts = {
        "mu": mu_r,
        "w0": row(lw["w0"]), "w2": _pad_rows(lw["w2"], 128).astype(BF16),
        "a0": row(lw["a0"]), "a2": _pad_rows(lw["a2"], 128).astype(BF16),
        "g2": _pad_rows(lw["g2"], 256).astype(BF16),
        "k_k": row(lw["k_k"]), "k_a": row(lw["k_a"]), "r_k": row(lw["r_k"]),
        "lnx_g": row(lw["lnx_g"]), "lnx_b": row(lw["lnx_b"]),
        "bd": (ch[:, None] == ch[None, :]).astype(BF16),
        "tri": ((rr[:, None] // CHUNK == rr[None, :] // CHUNK) & (rr[None, :] <= rr[:, None])).astype(BF16),
        "hmask": jnp.broadcast_to(jnp.stack([first, 1 - first])[:, None, :], (2, CHUNK, PAIR)),
        "strict": ((pp[None, :] % CHUNK) < tt[:, None]).astype(F32),
        "incl": ((pp[None, :] % CHUNK) <= tt[:, None]).astype(F32),
        "eye": ((pp[None, :] % CHUNK) == tt[:, None]).astype(F32),
    }

    u0, *streams, el = _front_call(x, mod, row(lw["mix_pre_g"]), w_in_r, consts, 2 * cw, tiles["inproj"])
    conv_w = jnp.broadcast_to(lw["conv_dw_w"][:, None, :], (CONV_K, SUBLANES, cw))
    y = _rwkv_call(streams, el, consts, tiles["rwkv"])
    x1, h2 = _mixout_call(u0, conv_w, row(lw["conv_dw_b"]), row(lw["conv_ln_g"]), row(lw["conv_ln_b"]), y, x,
                          mod, row(lw["mix_post_g"]), row(lw["ffn_pre_g"]), lw["w_out"].astype(BF16),
                          tiles["mixout"])
    return _ffn_call(x1, h2, mod, row(lw["ffn_post_g"]), lw["w_up"].astype(BF16), lw["ffn_dw_w"],
                     row(lw["ffn_dw_b"]), lw["w_down"].astype(BF16), tiles["ffn"])


def _tiles(t):
    pick = lambda pref: min(pref, t)
    return {"inproj": pick(512), "rwkv": pick(512), "mixout": pick(256), "ffn": pick(256)}


def kernel(x, c, ada_w, ada_b, mix_pre_g, mix_post_g, w_in, conv_dw_w, conv_dw_b, conv_ln_g,
           conv_ln_b, rwkv_mu, w0, w2, a0, a2, g2, k_k, k_a, r_k, lnx_g, lnx_b, w_out, ffn_pre_g,
           ffn_post_g, w_up, ffn_dw_w, ffn_dw_b, w_down):
    b, t, d = x.shape
    depth = ada_w.shape[0]
    tiles = _tiles(t)
    c8 = jnp.pad(c, ((0, 8 - b), (0, 0)))
    for l in range(depth):
        mod = _mod_call(c8, ada_w[l], ada_b[l].reshape(1, -1), 768)[:b].reshape(b, 6, d)
        lw = {"mix_pre_g": mix_pre_g[l], "mix_post_g": mix_post_g[l], "w_in": w_in[l],
              "conv_dw_w": conv_dw_w[l], "conv_dw_b": conv_dw_b[l], "conv_ln_g": conv_ln_g[l],
              "conv_ln_b": conv_ln_b[l], "rwkv_mu": rwkv_mu[l], "w0": w0[l], "w2": w2[l],
              "a0": a0[l], "a2": a2[l], "g2": g2[l], "k_k": k_k[l], "k_a": k_a[l],
              "r_k": r_k[l].reshape(-1), "lnx_g": lnx_g[l], "lnx_b": lnx_b[l], "w_out": w_out[l],
              "ffn_pre_g": ffn_pre_g[l], "ffn_post_g": ffn_post_g[l], "w_up": w_up[l],
              "ffn_dw_w": ffn_dw_w[l], "ffn_dw_b": ffn_dw_b[l], "w_down": w_down[l]}
        x = _layer(x, mod, lw, tiles)
    return x
```

```python
import functools
import math

import jax
import jax.numpy as jnp
from jax import lax
from jax.experimental import pallas as pl
from jax.experimental.pallas import tpu as pltpu

F32 = jnp.float32
BF16 = jnp.bfloat16

HEAD = 64
CHUNK = 64
PAIR = 2 * HEAD
CONV_K = 31
FFN_K = 3
RMS_EPS = 1e-6
LN_EPS = 1e-5
LNX_EPS = 64e-5
SUBLANES = 8
CONV_HALO = 32
FF_CHUNK = 256
A_GROUP = 4
SUB = 256
VMEM_LIMIT = 56 * 1024 * 1024

NT = (((1,), (1,)), ((), ()))
TN = (((0,), (0,)), ((), ()))


def _dot(a, b):
    return jnp.dot(a, b, preferred_element_type=F32)


def _dg(a, b, dims):
    return lax.dot_general(a, b, dims, preferred_element_type=F32)


def _sigmoid(x):
    return jax.nn.sigmoid(x)


def _rms(x, g):
    ms = jnp.mean(x * x, axis=-1, keepdims=True)
    return x * lax.rsqrt(ms + RMS_EPS) * g


def _mod_kernel(c_ref, w_ref, b_ref, o_ref):
    cs = c_ref[...]
    s = cs * _sigmoid(cs)
    o_ref[...] = jnp.dot(s, w_ref[...], preferred_element_type=F32,
                         precision=lax.Precision.HIGHEST) + b_ref[...]


def _mod_call(c8, ada_w, ada_b, tn):
    d, n = ada_w.shape
    return pl.pallas_call(
        _mod_kernel,
        grid=(n // tn,),
        in_specs=[pl.BlockSpec((8, d), lambda j: (0, 0)),
                  pl.BlockSpec((d, tn), lambda j: (0, j)),
                  pl.BlockSpec((1, tn), lambda j: (0, j))],
        out_specs=pl.BlockSpec((8, tn), lambda j: (0, j)),
        out_shape=jax.ShapeDtypeStruct((8, n), F32),
        compiler_params=pltpu.CompilerParams(dimension_semantics=("arbitrary",),
                                             vmem_limit_bytes=VMEM_LIMIT),
        name="adaln_mod",
    )(c8, ada_w, ada_b)


def _front_kernel(x_ref, mod_ref, g_ref, w_ref, mu_ref, mul_ref, w0_ref, w2_ref, a0_ref, a2_ref, g2_ref,
                  kk_ref, ka_ref, rk_ref, bd_ref, tri_ref,
                  u0_ref, kt_ref, rt_ref, kh_ref, bh_ref, v_ref, bonus_ref, gate_ref, el_ref,
                  prev_ref, prevl_ref):
    i = pl.program_id(1)
    tm = x_ref.shape[1]
    cw = u0_ref.shape[2]
    nc = 2 * cw
    rw = v_ref.shape[2]

    @pl.when(i == 0)
    def _():
        prev_ref[...] = jnp.zeros_like(prev_ref)
        prevl_ref[...] = jnp.zeros_like(prevl_ref)

    bd = bd_ref[...]
    tri = tri_ref[...]
    rows = lax.broadcasted_iota(jnp.int32, (SUB, 1), 0)

    def shifted(p, prev_row, mu):
        prev = jnp.where(rows == 0, prev_row[0:1, :], pltpu.roll(p, 1, axis=0))
        prev_row[0:1, :] = p[SUB - 1:SUB, :]
        return p + (prev - p) * mu

    def prep(p, pl_, r0):
        xs = shifted(p, prev_ref, mu_ref[...])
        xl = shifted(pl_, prevl_ref, mul_ref[...])
        r = xs[:, 0:rw]
        k = xs[:, rw:2 * rw]
        v = xs[:, 2 * rw:3 * rw]
        dl = w2_ref.shape[0]
        al = a2_ref.shape[0]
        wd = xl[:, 0:dl]
        ad = xl[:, dl:dl + al]
        gd = xl[:, dl + al:]
        zw = w0_ref[...] + _dot(jnp.tanh(wd).astype(BF16), w2_ref[...])
        lw = -math.exp(-0.5) * _sigmoid(zw)
        a = _sigmoid(a0_ref[...] + _dot(ad.astype(BF16), a2_ref[...]))
        gate = _dot(_sigmoid(gd).astype(BF16), g2_ref[...])
        kkr = k * kk_ref[...]
        ss = _dot((kkr * kkr).astype(BF16), bd)
        kk = kkr * lax.rsqrt(jnp.maximum(ss, 1e-24))
        km = k * (1.0 + (a - 1.0) * ka_ref[...])
        bonus = _dot((r * km * rk_ref[...]).astype(BF16), bd) * v
        hi = lw.astype(BF16)
        lo = (lw - hi.astype(F32)).astype(BF16)
        c = _dot(tri, hi) + _dot(tri, lo)
        ec = jnp.exp(c)
        ecn = jnp.exp(-c)
        rs = slice(r0, r0 + SUB)
        kt_ref[0, rs, :] = (kk * jnp.exp(c - lw)).astype(BF16)
        rt_ref[0, rs, :] = (r * ec).astype(BF16)
        kh_ref[0, rs, :] = (km * ecn).astype(BF16)
        bh_ref[0, rs, :] = (a * kk * ecn).astype(BF16)
        v_ref[0, rs, :] = v.astype(BF16)
        bonus_ref[0, rs, :] = bonus.astype(BF16)
        gate_ref[0, rs, :] = gate.astype(BF16)
        for j in range(SUB // CHUNK):
            last = ec[(j + 1) * CHUNK - 1:(j + 1) * CHUNK, :]
            el_ref[0, r0 // CHUNK + j] = jnp.broadcast_to(last, (SUBLANES, rw))

    n_main = nc + 3 * rw

    def proj(r0):
        x = x_ref[0, r0:r0 + SUB, :]
        h = (_rms(x, g_ref[...]) * (1.0 + mod_ref[0, 1:2, :]) + mod_ref[0, 0:1, :]).astype(BF16)
        return _dot(h, w_ref[:, 0:n_main]), _dot(h, w_ref[:, n_main:])

    p_next = proj(0)
    for r0 in range(0, tm, SUB):
        p = p_next
        if r0 + SUB < tm:
            p_next = proj(r0 + SUB)
        u0_ref[0, r0:r0 + SUB, :] = (p[0][:, :cw] * _sigmoid(p[0][:, cw:nc])).astype(BF16)
        prep(p[0][:, nc:], p[1], r0)


def _front_call(x, mod, g, w, consts, n_conv, tm):
    b, t, d = x.shape
    n = w.shape[1]
    rw = consts["w0"].shape[1]
    names = ["mu", "mu_lora", "w0", "w2", "a0", "a2", "g2", "k_k", "k_a", "r_k", "bd", "tri"]
    ops = [consts[nm] for nm in names]
    full = lambda arr: pl.BlockSpec(arr.shape, lambda bi, i: (0,) * arr.ndim)
    tok = lambda width: pl.BlockSpec((1, tm, width), lambda bi, i: (bi, i, 0))
    tok_shape = lambda width, dt: jax.ShapeDtypeStruct((b, t, width), dt)
    return pl.pallas_call(
        _front_kernel,
        grid=(b, t // tm),
        in_specs=[tok(d), pl.BlockSpec((1, 6, d), lambda bi, i: (bi, 0, 0)),
                  pl.BlockSpec((1, d), lambda bi, i: (0, 0)),
                  pl.BlockSpec((d, n), lambda bi, i: (0, 0))] + [full(a) for a in ops],
        out_specs=[tok(n_conv // 2)] + [tok(rw)] * 7
                  + [pl.BlockSpec((1, tm // CHUNK, SUBLANES, rw), lambda bi, i: (bi, i, 0, 0))],
        out_shape=[tok_shape(n_conv // 2, BF16)] + [tok_shape(rw, BF16)] * 7
                  + [jax.ShapeDtypeStruct((b, t // CHUNK, SUBLANES, rw), F32)],
        scratch_shapes=[pltpu.VMEM((SUBLANES, 3 * rw), F32),
                        pltpu.VMEM((SUBLANES, n - n_conv - 3 * rw), F32)],
        compiler_params=pltpu.CompilerParams(dimension_semantics=("arbitrary", "arbitrary"),
                                             vmem_limit_bytes=VMEM_LIMIT),
        name="inproj_prep",
    )(x, mod, g, w, *ops)


def _mixout_kernel(u0_ref, halo_ref, cwt_ref, cb_ref, lg_ref, lb_ref, y_ref, x_ref, mod_ref, g_ref, gf_ref,
                   w_ref, o_ref, h_ref, scr, shifted, u_scr, my_scr, *, rb):
    i = pl.program_id(1)
    tm = u0_ref.shape[1]
    cw = y_ref.shape[2]
    my_scr[...] = _dot(y_ref[0], w_ref[cw:, :])

    scr[CONV_HALO:CONV_HALO + tm, :] = u0_ref[0].astype(F32)
    scr[0:CONV_HALO, :] = jnp.where(i > 0, halo_ref[0].astype(F32), 0.0)
    span = tm + CONV_HALO - SUBLANES
    for r in range(1, SUBLANES):
        shifted[r - 1, 0:span, :] = scr[r:r + span, :]
    off = CONV_HALO - (CONV_K - 1)
    for r0 in range(0, tm, rb):
        acc = jnp.zeros((rb // SUBLANES, SUBLANES, cw), F32) + cb_ref[...]
        for k in range(CONV_K):
            r = (off + k) % SUBLANES
            a0 = r0 + off + k - r
            src = scr[a0:a0 + rb, :] if r == 0 else shifted[r - 1, a0:a0 + rb, :]
            acc = acc + cwt_ref[k] * src.reshape(rb // SUBLANES, SUBLANES, cw)
        acc = acc.reshape(rb, cw)
        mu = jnp.mean(acc, axis=-1, keepdims=True)
        dd = acc - mu
        var = jnp.mean(dd * dd, axis=-1, keepdims=True)
        z = dd * lax.rsqrt(var + LN_EPS) * lg_ref[...] + lb_ref[...]
        u_scr[r0:r0 + rb, :] = (z * _sigmoid(z)).astype(BF16)

    m = _dot(u_scr[...], w_ref[0:cw, :]) + my_scr[...]
    x1 = x_ref[0] + mod_ref[0, 2:3, :] * _rms(m, g_ref[...])
    o_ref[0] = x1
    h_ref[0] = (_rms(x1, gf_ref[...]) * (1.0 + mod_ref[0, 4:5, :]) + mod_ref[0, 3:4, :]).astype(h_ref.dtype)


def _mixout_call(u0, conv_w, conv_b, lg, lb, y, x, mod, g, gf, w, tm, rb=32):
    b, t, d = x.shape
    cw = u0.shape[2]
    hb = tm // CONV_HALO
    row_blk = lambda width: pl.BlockSpec((1, tm, width), lambda bi, i: (bi, i, 0))
    vec = lambda width: pl.BlockSpec((1, width), lambda bi, i: (0, 0))
    return pl.pallas_call(
        functools.partial(_mixout_kernel, rb=rb),
        grid=(b, t // tm),
        in_specs=[row_blk(cw),
                  pl.BlockSpec((1, CONV_HALO, cw), lambda bi, i: (bi, jnp.maximum(i * hb - 1, 0), 0)),
                  pl.BlockSpec((CONV_K, SUBLANES, cw), lambda bi, i: (0, 0, 0)),
                  vec(cw), vec(cw), vec(cw),
                  row_blk(y.shape[2]), row_blk(d),
                  pl.BlockSpec((1, 6, d), lambda bi, i: (bi, 0, 0)),
                  vec(d), vec(d),
                  pl.BlockSpec(w.shape, lambda bi, i: (0, 0))],
        out_specs=[row_blk(d), row_blk(d)],
        out_shape=[jax.ShapeDtypeStruct((b, t, d), F32), jax.ShapeDtypeStruct((b, t, d), BF16)],
        scratch_shapes=[pltpu.VMEM((tm + CONV_HALO, cw), F32),
                        pltpu.VMEM((SUBLANES - 1, tm + CONV_HALO - SUBLANES, cw), F32),
                        pltpu.VMEM((tm, cw), BF16),
                        pltpu.VMEM((tm, d), F32)],
        compiler_params=pltpu.CompilerParams(dimension_semantics=("arbitrary", "arbitrary"),
                                             vmem_limit_bytes=VMEM_LIMIT),
        name="conv_outproj",
    )(u0, u0, conv_w, conv_b, lg, lb, y, x, mod, g, gf, w)


def _stack(val, m0, m1):
    return jnp.concatenate([val * m0, val * m1], axis=0)


def _neumann_inverse(n_w, eye_w, m0, m1):
    levels = CHUNK.bit_length() - 2
    t_w = [eye_w + n_ for n_ in n_w]
    pb = [n_.astype(BF16) for n_ in n_w]
    pk = [_dot(x, _stack(x, m0, m1)) for x in pb]
    for k in range(1, levels + 1):
        pkb = [x.astype(BF16) for x in pk]
        pd = [_stack(x, m0, m1) for x in pkb]
        if k < levels:
            both = [_dot(jnp.concatenate([t_.astype(BF16), p_], axis=0), d_)
                    for t_, p_, d_ in zip(t_w, pkb, pd)]
            t_w = [t_ + b_[:CHUNK] for t_, b_ in zip(t_w, both)]
            pk = [b_[CHUNK:] for b_ in both]
        else:
            t_w = [t_ + _dot(t_.astype(BF16), d_) for t_, d_ in zip(t_w, pd)]
    return t_w


def _rwkv_kernel(kt_ref, rt_ref, kh_ref, bh_ref, v_ref, bonus_ref, gate_ref, el_ref,
                 lg_ref, lb_ref, bd_ref, hmask_ref, strict_ref, incl_ref, eye_ref,
                 o_ref,
                 s_ref, y_ref, ak_ref, abr_ref, t_ref):
    i = pl.program_id(0)
    nb, tb, rw = o_ref.shape
    npair = rw // PAIR
    cpb = tb // CHUNK

    @pl.when(i == 0)
    def _():
        s_ref[...] = jnp.zeros_like(s_ref)

    strict = strict_ref[...]
    incl = incl_ref[...]
    eye = eye_ref[...]
    m0 = hmask_ref[0]
    m1 = hmask_ref[1]
    pairs = range(npair)
    lanes = [slice(pi * PAIR, (pi + 1) * PAIR) for pi in pairs]
    stack = lambda val: _stack(val, m0, m1)

    def rows_of(j):
        return pl.ds(pl.multiple_of(j * CHUNK, CHUNK), CHUNK)

    def phase_a(gi, carry):
        inst = [(bi, gi * A_GROUP + j, pi) for bi in range(nb) for j in range(A_GROUP) for pi in pairs]
        sl = [(bi, rows_of(j), lanes[pi]) for bi, j, pi in inst]
        lhs = [jnp.concatenate([kt_ref[x], rt_ref[x]], axis=0) for x in sl]
        kb = [jnp.concatenate([stack(kh_ref[x]), stack(bh_ref[x])], axis=0) for x in sl]
        g = [_dg(l_, k_, NT) for l_, k_ in zip(lhs, kb)]
        for g_, (bi, j, pi) in zip(g, inst):
            ci = bi * cpb + j
            ak_ref[ci, pi, 0:CHUNK] = (g_[0:CHUNK, 0:PAIR] * strict).astype(BF16)
            ak_ref[ci, pi, CHUNK:] = (g_[CHUNK:, 0:PAIR] * incl).astype(BF16)
            abr_ref[ci, pi] = (g_[CHUNK:, PAIR:] * incl).astype(BF16)
        t_w = _neumann_inverse([-(g_[0:CHUNK, PAIR:] * strict) for g_ in g], eye, m0, m1)
        for t_, (bi, j, pi) in zip(t_w, inst):
            t_ref[bi * cpb + j, pi] = t_.astype(BF16)
        return carry

    lax.fori_loop(0, cpb // A_GROUP, phase_a, 0)

    def phase_b(j, carry):
        chains = [(bi, pi) for bi in range(nb) for pi in pairs]
        cis = [bi * cpb + j for bi, _ in chains]
        rs = rows_of(j)
        sl = [(bi, rs, lanes[pi]) for bi, pi in chains]
        lhs = [jnp.concatenate([kt_ref[x], rt_ref[x]], axis=0) for x in sl]
        kb = [jnp.concatenate([stack(kh_ref[x]), -stack(bh_ref[x])], axis=0) for x in sl]
        vs = [stack(v_ref[x]) for x in sl]
        s = [s_ref[bi * npair + pi] for bi, pi in chains]
        sb = [x.astype(BF16) for x in s]
        xy = [_dg(l_, sb_, NT) + _dot(ak_ref[ci, pi], v_)
              for l_, sb_, v_, ci, (_, pi) in zip(lhs, sb, vs, cis, chains)]
        us = [stack(_dot(t_ref[ci, pi], stack(x_[0:CHUNK].astype(BF16))).astype(BF16))
              for x_, ci, (_, pi) in zip(xy, cis, chains)]
        ds_ = [_dg(jnp.concatenate([v_, u_], axis=0), k_, TN) for v_, u_, k_ in zip(vs, us, kb)]
        for s_, d_, (bi, pi) in zip(s, ds_, chains):
            e_l = el_ref[bi, j][0:1, lanes[pi]]
            s_ref[bi * npair + pi] = (s_ + d_) * e_l
        for x_, u_, x, ci, (_, pi) in zip(xy, us, sl, cis, chains):
            y_ref[x] = x_[CHUNK:] - _dot(abr_ref[ci, pi], u_)
        return carry

    lax.fori_loop(0, cpb, phase_b, 0, unroll=4)

    bd = bd_ref[...]
    y = y_ref[...].reshape(nb * tb, rw)
    inv = 1.0 / HEAD
    mean = _dot(y.astype(BF16), bd) * inv
    d = y - mean
    var = _dot((d * d).astype(BF16), bd) * inv
    yl = d * lax.rsqrt(var + LNX_EPS) * lg_ref[...] + lb_ref[...]
    bonus = bonus_ref[...].reshape(nb * tb, rw).astype(F32)
    gate = gate_ref[...].reshape(nb * tb, rw).astype(F32)
    o_ref[...] = ((yl + bonus) * gate).astype(o_ref.dtype).reshape(nb, tb, rw)


def _rwkv_call(streams, el, consts, tb):
    b, t, rw = streams[0].shape
    npair = rw // PAIR
    names = ["lnx_g", "lnx_b", "bd", "hmask", "strict", "incl", "eye"]
    ops = [consts[nm] for nm in names]
    full = lambda arr: pl.BlockSpec(arr.shape, lambda i: (0,) * arr.ndim)
    tok = pl.BlockSpec((b, tb, rw), lambda i: (0, i, 0))
    return pl.pallas_call(
        _rwkv_kernel,
        grid=(t // tb,),
        in_specs=[tok] * len(streams)
                 + [pl.BlockSpec((b, tb // CHUNK, SUBLANES, rw), lambda i: (0, i, 0, 0))]
                 + [full(a) for a in ops],
        out_specs=tok,
        out_shape=jax.ShapeDtypeStruct((b, t, rw), BF16),
        scratch_shapes=[pltpu.VMEM((b * npair, PAIR, PAIR), F32), pltpu.VMEM((b, tb, rw), F32)]
                       + [pltpu.VMEM((b * tb // CHUNK, npair, 2 * CHUNK, PAIR), BF16)]
                       + [pltpu.VMEM((b * tb // CHUNK, npair, CHUNK, PAIR), BF16)] * 2,
        compiler_params=pltpu.CompilerParams(dimension_semantics=("arbitrary",),
                                             vmem_limit_bytes=VMEM_LIMIT),
        name="rwkv_group",
    )(*streams, el, *ops)


def _ffn_kernel(x_ref, h_ref, mod_ref, gpost_ref, wup_ref, cw_ref, cb_ref, wdn_ref,
                o_ref, z_ref, zc_ref, acc_ref):
    i = pl.program_id(1)
    tm = x_ref.shape[1]
    d_ff = wdn_ref.shape[0]
    hb = h_ref[0]

    @pl.when(i == 0)
    def _():
        zc_ref[...] = jnp.zeros_like(zc_ref)

    n_chunks = d_ff // FF_CHUNK
    cols = lambda f: (slice(f * FF_CHUNK, (f + 1) * FF_CHUNK),
                      slice(d_ff + f * FF_CHUNK, d_ff + (f + 1) * FF_CHUNK))

    def up(f):
        cg, cv = cols(f)
        return jnp.concatenate([_dot(hb, wup_ref[:, cg]), _dot(hb, wup_ref[:, cv])], axis=1)

    def conv_act(f, z):
        cg, cv = cols(f)
        zb = z_ref.at[f % 2]
        zb[0:SUBLANES, :] = zc_ref[f]
        zb[SUBLANES:, :] = z
        zc_ref[f] = z[tm - SUBLANES:, :]
        zc = jnp.concatenate([cb_ref[:, cg], cb_ref[:, cv]], axis=1)
        for k in range(FFN_K):
            o = SUBLANES - (FFN_K - 1) + k
            wk = jnp.concatenate([cw_ref[k:k + 1, cg], cw_ref[k:k + 1, cv]], axis=1)
            zc = zc + wk * zb[o:o + tm, :]
        zg = zc[:, :FF_CHUNK]
        return (zg * _sigmoid(zg) * zc[:, FF_CHUNK:]).astype(BF16)

    act = conv_act(0, up(0))
    z_next = up(1)
    for f in range(n_chunks):
        z_cur = z_next
        if f + 2 < n_chunks:
            z_next = up(f + 2)
        contrib = _dot(act, wdn_ref[f * FF_CHUNK:(f + 1) * FF_CHUNK, :])
        if f == 0:
            acc_ref[...] = contrib
        else:
            acc_ref[...] += contrib
        if f + 1 < n_chunks:
            act = conv_act(f + 1, z_cur)
    o_ref[0] = x_ref[0] + mod_ref[0, 5:6, :] * _rms(acc_ref[...], gpost_ref[...])


def _ffn_call(x1, h2, mod, gpost, wup, cw, cb, wdn, tm):
    b, t, d = x1.shape
    full = lambda arr: pl.BlockSpec(arr.shape, lambda bi, i: (0,) * arr.ndim)
    return pl.pallas_call(
        _ffn_kernel,
        grid=(b, t // tm),
        in_specs=[pl.BlockSpec((1, tm, d), lambda bi, i: (bi, i, 0)),
                  pl.BlockSpec((1, tm, d), lambda bi, i: (bi, i, 0)),
                  pl.BlockSpec((1, 6, d), lambda bi, i: (bi, 0, 0)),
                  full(gpost), full(wup), full(cw), full(cb), full(wdn)],
        out_specs=pl.BlockSpec((1, tm, d), lambda bi, i: (bi, i, 0)),
        out_shape=jax.ShapeDtypeStruct((b, t, d), F32),
        scratch_shapes=[pltpu.VMEM((2, tm + SUBLANES, 2 * FF_CHUNK), F32),
                        pltpu.VMEM((wdn.shape[0] // FF_CHUNK, SUBLANES, 2 * FF_CHUNK), F32),
                        pltpu.VMEM((tm, d), F32)],
        compiler_params=pltpu.CompilerParams(dimension_semantics=("arbitrary", "arbitrary"),
                                             vmem_limit_bytes=VMEM_LIMIT),
        name="conv_ffn",
    )(x1, h2, mod, gpost, wup, cw, cb, wdn)


def _layer(x, mod, lw, tiles):
    b, t, d = x.shape
    cw = lw["conv_dw_w"].shape[1]
    rw = lw["w0"].shape[0]
    row = lambda vec: vec.reshape(1, -1)

    mu = row(lw["rwkv_mu"])
    ch = jnp.arange(rw) // HEAD
    rr = jnp.arange(SUB)
    pp = jnp.arange(PAIR)
    tt = jnp.arange(CHUNK)
    first = (pp < HEAD).astype(BF16)
    consts = {
        "mu": mu[:, :3 * rw], "mu_lora": mu[:, 3 * rw:],
        "w0": row(lw["w0"]), "w2": lw["w2"].astype(BF16),
        "a0": row(lw["a0"]), "a2": lw["a2"].astype(BF16),
        "g2": lw["g2"].astype(BF16),
        "k_k": row(lw["k_k"]), "k_a": row(lw["k_a"]), "r_k": row(lw["r_k"]),
        "lnx_g": row(lw["lnx_g"]), "lnx_b": row(lw["lnx_b"]),
        "bd": (ch[:, None] == ch[None, :]).astype(BF16),
        "tri": ((rr[:, None] // CHUNK == rr[None, :] // CHUNK) & (rr[None, :] <= rr[:, None])).astype(BF16),
        "hmask": jnp.broadcast_to(jnp.stack([first, 1 - first])[:, None, :], (2, CHUNK, PAIR)),
        "strict": ((pp[None, :] % CHUNK) < tt[:, None]).astype(F32),
        "incl": ((pp[None, :] % CHUNK) <= tt[:, None]).astype(F32),
        "eye": ((pp[None, :] % CHUNK) == tt[:, None]).astype(F32),
    }

    u0, *streams, el = _front_call(x, mod, row(lw["mix_pre_g"]), lw["w_in"].astype(BF16), consts, 2 * cw, tiles["inproj"])
    conv_w = jnp.broadcast_to(lw["conv_dw_w"][:, None, :], (CONV_K, SUBLANES, cw))
    y = _rwkv_call(streams, el, consts, tiles["rwkv"])
    x1, h2 = _mixout_call(u0, conv_w, row(lw["conv_dw_b"]), row(lw["conv_ln_g"]), row(lw["conv_ln_b"]), y, x,
                          mod, row(lw["mix_post_g"]), row(lw["ffn_pre_g"]), lw["w_out"].astype(BF16),
                          tiles["mixout"])
    return _ffn_call(x1, h2, mod, row(lw["ffn_post_g"]), lw["w_up"].astype(BF16), lw["ffn_dw_w"],
                     row(lw["ffn_dw_b"]), lw["w_down"].astype(BF16), tiles["ffn"])


def _tiles(t):
    pick = lambda pref: min(pref, t)
    return {"inproj": pick(512), "rwkv": pick(512), "mixout": pick(256), "ffn": pick(256)}


def kernel(x, c, ada_w, ada_b, mix_pre_g, mix_post_g, w_in, conv_dw_w, conv_dw_b, conv_ln_g,
           conv_ln_b, rwkv_mu, w0, w2, a0, a2, g2, k_k, k_a, r_k, lnx_g, lnx_b, w_out, ffn_pre_g,
           ffn_post_g, w_up, ffn_dw_w, ffn_dw_b, w_down):
    b, t, d = x.shape
    depth = ada_w.shape[0]
    tiles = _tiles(t)
    c8 = jnp.pad(c, ((0, 8 - b), (0, 0)))
    for l in range(depth):
        mod = _mod_call(c8, ada_w[l], ada_b[l].reshape(1, -1), 768)[:b].reshape(b, 6, d)
        lw = {"mix_pre_g": mix_pre_g[l], "mix_post_g": mix_post_g[l], "w_in": w_in[l],
              "conv_dw_w": conv_dw_w[l], "conv_dw_b": conv_dw_b[l], "conv_ln_g": conv_ln_g[l],
              "conv_ln_b": conv_ln_b[l], "rwkv_mu": rwkv_mu[l], "w0": w0[l], "w2": w2[l],
              "a0": a0[l], "a2": a2[l], "g2": g2[l], "k_k": k_k[l], "k_a": k_a[l],
              "r_k": r_k[l].reshape(-1), "lnx_g": lnx_g[l], "lnx_b": lnx_b[l], "w_out": w_out[l],
              "ffn_pre_g": ffn_pre_g[l], "ffn_post_g": ffn_post_g[l], "w_up": w_up[l],
              "ffn_dw_w": ffn_dw_w[l], "ffn_dw_b": ffn_dw_b[l], "w_down": w_down[l]}
        x = _layer(x, mod, lw, tiles)
    return x
```

```python
import functools
import math

import jax
import jax.numpy as jnp
from jax import lax
from jax.experimental import pallas as pl
from jax.experimental.pallas import tpu as pltpu

F32 = jnp.float32
BF16 = jnp.bfloat16

HEAD = 64
CHUNK = 64
PAIR = 2 * HEAD
CONV_K = 31
FFN_K = 3
RMS_EPS = 1e-6
LN_EPS = 1e-5
LNX_EPS = 64e-5
SUBLANES = 8
CONV_HALO = 32
FF_CHUNK = 256
A_GROUP = 4
SUB = 256
VMEM_LIMIT = 56 * 1024 * 1024

NT = (((1,), (1,)), ((), ()))
TN = (((0,), (0,)), ((), ()))


def _dot(a, b):
    return jnp.dot(a, b, preferred_element_type=F32)


def _dg(a, b, dims):
    return lax.dot_general(a, b, dims, preferred_element_type=F32)


def _sigmoid(x):
    return jax.nn.sigmoid(x)


def _rms(x, g):
    ms = jnp.mean(x * x, axis=-1, keepdims=True)
    return x * lax.rsqrt(ms + RMS_EPS) * g


def _mod_kernel(ct_ref, w_ref, b_ref, o_ref, *, nb):
    k = pl.program_id(0)
    cs = ct_ref[...]
    s = cs * _sigmoid(cs)
    w = w_ref[...]

    @pl.when(k == 0)
    def _():
        o_ref[...] = jnp.broadcast_to(b_ref[...], o_ref.shape)

    for bi in range(nb):
        o_ref[bi:bi + 1, :] += jnp.sum(w * s[:, bi:bi + 1], axis=0, keepdims=True)


def _mod_call(c8, ada_w, ada_b, tk, nb):
    d, n = ada_w.shape
    return pl.pallas_call(
        functools.partial(_mod_kernel, nb=nb),
        grid=(d // tk,),
        in_specs=[pl.BlockSpec((tk, 8), lambda k: (k, 0)),
                  pl.BlockSpec((tk, n), lambda k: (k, 0)),
                  pl.BlockSpec((1, n), lambda k: (0, 0))],
        out_specs=pl.BlockSpec((8, n), lambda k: (0, 0)),
        out_shape=jax.ShapeDtypeStruct((8, n), F32),
        compiler_params=pltpu.CompilerParams(dimension_semantics=("arbitrary",),
                                             vmem_limit_bytes=VMEM_LIMIT),
        name="adaln_mod",
    )(c8, ada_w, ada_b)


def _front_kernel(x_ref, mod_ref, g_ref, w_ref, mu_ref, mul_ref, w0_ref, w2_ref, a0_ref, a2_ref, g2_ref,
                  kk_ref, ka_ref, rk_ref, bd_ref, tri_ref,
                  u0_ref, kt_ref, rt_ref, kh_ref, bh_ref, v_ref, bonus_ref, gate_ref, el_ref,
                  prev_ref, prevl_ref):
    i = pl.program_id(1)
    tm = x_ref.shape[1]
    cw = u0_ref.shape[2]
    nc = 2 * cw
    rw = v_ref.shape[2]

    @pl.when(i == 0)
    def _():
        prev_ref[...] = jnp.zeros_like(prev_ref)
        prevl_ref[...] = jnp.zeros_like(prevl_ref)

    bd = bd_ref[...]
    tri = tri_ref[...]
    rows = lax.broadcasted_iota(jnp.int32, (SUB, 1), 0)

    def shifted(p, prev_row, mu):
        prev = jnp.where(rows == 0, prev_row[0:1, :], pltpu.roll(p, 1, axis=0))
        prev_row[0:1, :] = p[SUB - 1:SUB, :]
        return p + (prev - p) * mu

    def prep(p, pl_, r0):
        xs = shifted(p, prev_ref, mu_ref[...])
        xl = shifted(pl_, prevl_ref, mul_ref[...])
        r = xs[:, 0:rw]
        k = xs[:, rw:2 * rw]
        v = xs[:, 2 * rw:3 * rw]
        dl = w2_ref.shape[0]
        al = a2_ref.shape[0]
        wd = xl[:, 0:dl]
        ad = xl[:, dl:dl + al]
        gd = xl[:, dl + al:]
        zw = w0_ref[...] + _dot(jnp.tanh(wd).astype(BF16), w2_ref[...])
        lw = -math.exp(-0.5) * _sigmoid(zw)
        a = _sigmoid(a0_ref[...] + _dot(ad.astype(BF16), a2_ref[...]))
        gate = _dot(_sigmoid(gd).astype(BF16), g2_ref[...])
        kkr = k * kk_ref[...]
        ss = _dot((kkr * kkr).astype(BF16), bd)
        kk = kkr * lax.rsqrt(jnp.maximum(ss, 1e-24))
        km = k * (1.0 + (a - 1.0) * ka_ref[...])
        bonus = _dot((r * km * rk_ref[...]).astype(BF16), bd) * v
        hi = lw.astype(BF16)
        lo = (lw - hi.astype(F32)).astype(BF16)
        c = _dot(tri, hi) + _dot(tri, lo)
        ec = jnp.exp(c)
        ecn = jnp.exp(-c)
        rs = slice(r0, r0 + SUB)
        kt_ref[0, rs, :] = (kk * jnp.exp(c - lw)).astype(BF16)
        rt_ref[0, rs, :] = (r * ec).astype(BF16)
        kh_ref[0, rs, :] = (km * ecn).astype(BF16)
        bh_ref[0, rs, :] = (a * kk * ecn).astype(BF16)
        v_ref[0, rs, :] = v.astype(BF16)
        bonus_ref[0, rs, :] = bonus.astype(BF16)
        gate_ref[0, rs, :] = gate.astype(BF16)
        for j in range(SUB // CHUNK):
            last = ec[(j + 1) * CHUNK - 1:(j + 1) * CHUNK, :]
            el_ref[0, r0 // CHUNK + j] = jnp.broadcast_to(last, (SUBLANES, rw))

    n_main = nc + 3 * rw

    def proj(r0):
        x = x_ref[0, r0:r0 + SUB, :]
        h = (_rms(x, g_ref[...]) * (1.0 + mod_ref[0, 1:2, :]) + mod_ref[0, 0:1, :]).astype(BF16)
        return _dot(h, w_ref[:, 0:n_main]), _dot(h, w_ref[:, n_main:])

    p_next = proj(0)
    for r0 in range(0, tm, SUB):
        p = p_next
        if r0 + SUB < tm:
            p_next = proj(r0 + SUB)
        u0_ref[0, r0:r0 + SUB, :] = (p[0][:, :cw] * _sigmoid(p[0][:, cw:nc])).astype(BF16)
        prep(p[0][:, nc:], p[1], r0)


def _front_call(x, mod, g, w, consts, n_conv, tm):
    b, t, d = x.shape
    n = w.shape[1]
    rw = consts["w0"].shape[1]
    names = ["mu", "mu_lora", "w0", "w2", "a0", "a2", "g2", "k_k", "k_a", "r_k", "bd", "tri"]
    ops = [consts[nm] for nm in names]
    full = lambda arr: pl.BlockSpec(arr.shape, lambda bi, i: (0,) * arr.ndim)
    tok = lambda width: pl.BlockSpec((1, tm, width), lambda bi, i: (bi, i, 0))
    tok_shape = lambda width, dt: jax.ShapeDtypeStruct((b, t, width), dt)
    return pl.pallas_call(
        _front_kernel,
        grid=(b, t // tm),
        in_specs=[tok(d), pl.BlockSpec((1, 6, d), lambda bi, i: (bi, 0, 0)),
                  pl.BlockSpec((1, d), lambda bi, i: (0, 0)),
                  pl.BlockSpec((d, n), lambda bi, i: (0, 0))] + [full(a) for a in ops],
        out_specs=[tok(n_conv // 2)] + [tok(rw)] * 7
                  + [pl.BlockSpec((1, tm // CHUNK, SUBLANES, rw), lambda bi, i: (bi, i, 0, 0))],
        out_shape=[tok_shape(n_conv // 2, BF16)] + [tok_shape(rw, BF16)] * 7
                  + [jax.ShapeDtypeStruct((b, t // CHUNK, SUBLANES, rw), F32)],
        scratch_shapes=[pltpu.VMEM((SUBLANES, 3 * rw), F32),
                        pltpu.VMEM((SUBLANES, n - n_conv - 3 * rw), F32)],
        compiler_params=pltpu.CompilerParams(dimension_semantics=("arbitrary", "arbitrary"),
                                             vmem_limit_bytes=VMEM_LIMIT),
        name="inproj_prep",
    )(x, mod, g, w, *ops)


def _mixout_kernel(u0_ref, halo_ref, cwt_ref, cb_ref, lg_ref, lb_ref, y_ref, x_ref, mod_ref, g_ref, gf_ref,
                   w_ref, o_ref, h_ref, scr, shifted, u_scr, my_scr, *, rb):
    i = pl.program_id(1)
    tm = u0_ref.shape[1]
    cw = y_ref.shape[2]
    my_scr[...] = _dot(y_ref[0], w_ref[cw:, :])

    scr[CONV_HALO:CONV_HALO + tm, :] = u0_ref[0].astype(F32)
    scr[0:CONV_HALO, :] = jnp.where(i > 0, halo_ref[0].astype(F32), 0.0)
    span = tm + CONV_HALO - SUBLANES
    for r in range(1, SUBLANES):
        shifted[r - 1, 0:span, :] = scr[r:r + span, :]
    off = CONV_HALO - (CONV_K - 1)
    for r0 in range(0, tm, rb):
        acc = jnp.zeros((rb // SUBLANES, SUBLANES, cw), F32) + cb_ref[...]
        for k in range(CONV_K):
            r = (off + k) % SUBLANES
            a0 = r0 + off + k - r
            src = scr[a0:a0 + rb, :] if r == 0 else shifted[r - 1, a0:a0 + rb, :]
            acc = acc + cwt_ref[k] * src.reshape(rb // SUBLANES, SUBLANES, cw)
        acc = acc.reshape(rb, cw)
        mu = jnp.mean(acc, axis=-1, keepdims=True)
        dd = acc - mu
        var = jnp.mean(dd * dd, axis=-1, keepdims=True)
        z = dd * lax.rsqrt(var + LN_EPS) * lg_ref[...] + lb_ref[...]
        zb = z.astype(BF16)
        u_scr[r0:r0 + rb, :] = zb * _sigmoid(zb)

    m = _dot(u_scr[...], w_ref[0:cw, :]) + my_scr[...]
    x1 = x_ref[0] + mod_ref[0, 2:3, :] * _rms(m, g_ref[...])
    o_ref[0] = x1
    scale = (gf_ref[...] * (1.0 + mod_ref[0, 4:5, :])).astype(BF16)
    xn = (x1 * lax.rsqrt(jnp.mean(x1 * x1, axis=-1, keepdims=True) + RMS_EPS)).astype(BF16)
    h_ref[0] = xn * scale + mod_ref[0, 3:4, :].astype(BF16)


def _mixout_call(u0, conv_w, conv_b, lg, lb, y, x, mod, g, gf, w, tm, rb=16):
    b, t, d = x.shape
    cw = u0.shape[2]
    hb = tm // CONV_HALO
    row_blk = lambda width: pl.BlockSpec((1, tm, width), lambda bi, i: (bi, i, 0))
    vec = lambda width: pl.BlockSpec((1, width), lambda bi, i: (0, 0))
    return pl.pallas_call(
        functools.partial(_mixout_kernel, rb=rb),
        grid=(b, t // tm),
        in_specs=[row_blk(cw),
                  pl.BlockSpec((1, CONV_HALO, cw), lambda bi, i: (bi, jnp.maximum(i * hb - 1, 0), 0)),
                  pl.BlockSpec((CONV_K, SUBLANES, cw), lambda bi, i: (0, 0, 0)),
                  vec(cw), vec(cw), vec(cw),
                  row_blk(y.shape[2]), row_blk(d),
                  pl.BlockSpec((1, 6, d), lambda bi, i: (bi, 0, 0)),
                  vec(d), vec(d),
                  pl.BlockSpec(w.shape, lambda bi, i: (0, 0))],
        out_specs=[row_blk(d), row_blk(d)],
        out_shape=[jax.ShapeDtypeStruct((b, t, d), F32), jax.ShapeDtypeStruct((b, t, d), BF16)],
        scratch_shapes=[pltpu.VMEM((tm + CONV_HALO, cw), F32),
                        pltpu.VMEM((SUBLANES - 1, tm + CONV_HALO - SUBLANES, cw), F32),
                        pltpu.VMEM((tm, cw), BF16),
                        pltpu.VMEM((tm, d), F32)],
        compiler_params=pltpu.CompilerParams(dimension_semantics=("arbitrary", "arbitrary"),
                                             vmem_limit_bytes=VMEM_LIMIT),
        name="conv_outproj",
    )(u0, u0, conv_w, conv_b, lg, lb, y, x, mod, g, gf, w)


def _stack(val, m0, m1):
    return jnp.concatenate([val * m0, val * m1], axis=0)


def _neumann_inverse(n_w, eye_w, m0, m1):
    levels = CHUNK.bit_length() - 2
    t_w = [eye_w + n_ for n_ in n_w]
    pb = [n_.astype(BF16) for n_ in n_w]
    pk = [_dot(x, _stack(x, m0, m1)) for x in pb]
    for k in range(1, levels + 1):
        pkb = [x.astype(BF16) for x in pk]
        pd = [_stack(x, m0, m1) for x in pkb]
        if k < levels:
            both = [_dot(jnp.concatenate([t_.astype(BF16), p_], axis=0), d_)
                    for t_, p_, d_ in zip(t_w, pkb, pd)]
            t_w = [t_ + b_[:CHUNK] for t_, b_ in zip(t_w, both)]
            pk = [b_[CHUNK:] for b_ in both]
        else:
            t_w = [t_ + _dot(t_.astype(BF16), d_) for t_, d_ in zip(t_w, pd)]
    return t_w


def _rwkv_kernel(kt_ref, rt_ref, kh_ref, bh_ref, v_ref, bonus_ref, gate_ref, el_ref,
                 lg_ref, lb_ref, bd_ref, hmask_ref, strict_ref, incl_ref, eye_ref,
                 o_ref,
                 s_ref, y_ref, ak_ref, abr_ref, t_ref):
    i = pl.program_id(0)
    nb, tb, rw = o_ref.shape
    npair = rw // PAIR
    cpb = tb // CHUNK

    @pl.when(i == 0)
    def _():
        s_ref[...] = jnp.zeros_like(s_ref)

    strict = strict_ref[...]
    incl = incl_ref[...]
    eye = eye_ref[...]
    m0 = hmask_ref[0]
    m1 = hmask_ref[1]
    pairs = range(npair)
    lanes = [slice(pi * PAIR, (pi + 1) * PAIR) for pi in pairs]
    stack = lambda val: _stack(val, m0, m1)

    def rows_of(j):
        return pl.ds(pl.multiple_of(j * CHUNK, CHUNK), CHUNK)

    def phase_a(gi, carry):
        inst = [(bi, gi * A_GROUP + j, pi) for bi in range(nb) for j in range(A_GROUP) for pi in pairs]
        sl = [(bi, rows_of(j), lanes[pi]) for bi, j, pi in inst]
        lhs = [jnp.concatenate([kt_ref[x], rt_ref[x]], axis=0) for x in sl]
        kb = [jnp.concatenate([stack(kh_ref[x]), stack(bh_ref[x])], axis=0) for x in sl]
        g = [_dg(l_, k_, NT) for l_, k_ in zip(lhs, kb)]
        for g_, (bi, j, pi) in zip(g, inst):
            ci = bi * cpb + j
            ak_ref[ci, pi, 0:CHUNK] = (g_[0:CHUNK, 0:PAIR] * strict).astype(BF16)
            ak_ref[ci, pi, CHUNK:] = (g_[CHUNK:, 0:PAIR] * incl).astype(BF16)
            abr_ref[ci, pi] = (g_[CHUNK:, PAIR:] * incl).astype(BF16)
        t_w = _neumann_inverse([-(g_[0:CHUNK, PAIR:] * strict) for g_ in g], eye, m0, m1)
        for t_, (bi, j, pi) in zip(t_w, inst):
            t_ref[bi * cpb + j, pi] = t_.astype(BF16)
        return carry

    lax.fori_loop(0, cpb // A_GROUP, phase_a, 0)

    def phase_b(j, carry):
        chains = [(bi, pi) for bi in range(nb) for pi in pairs]
        cis = [bi * cpb + j for bi, _ in chains]
        rs = rows_of(j)
        sl = [(bi, rs, lanes[pi]) for bi, pi in chains]
        lhs = [jnp.concatenate([kt_ref[x], rt_ref[x]], axis=0) for x in sl]
        kb = [jnp.concatenate([stack(kh_ref[x]), -stack(bh_ref[x])], axis=0) for x in sl]
        vs = [stack(v_ref[x]) for x in sl]
        s = [s_ref[bi * npair + pi] for bi, pi in chains]
        sb = [x.astype(BF16) for x in s]
        xy = [_dg(l_, sb_, NT) + _dot(ak_ref[ci, pi], v_)
              for l_, sb_, v_, ci, (_, pi) in zip(lhs, sb, vs, cis, chains)]
        us = [stack(_dot(t_ref[ci, pi], stack(x_[0:CHUNK].astype(BF16))).astype(BF16))
              for x_, ci, (_, pi) in zip(xy, cis, chains)]
        ds_ = [_dg(jnp.concatenate([v_, u_], axis=0), k_, TN) for v_, u_, k_ in zip(vs, us, kb)]
        for s_, d_, (bi, pi) in zip(s, ds_, chains):
            e_l = el_ref[bi, j][0:1, lanes[pi]]
            s_ref[bi * npair + pi] = (s_ + d_) * e_l
        for x_, u_, x, ci, (_, pi) in zip(xy, us, sl, cis, chains):
            y_ref[x] = x_[CHUNK:] - _dot(abr_ref[ci, pi], u_)
        return carry

    lax.fori_loop(0, cpb, phase_b, 0, unroll=4)

    bd = bd_ref[...]
    y = y_ref[...].reshape(nb * tb, rw)
    inv = 1.0 / HEAD
    mean = _dot(y.astype(BF16), bd) * inv
    d = y - mean
    var = _dot((d * d).astype(BF16), bd) * inv
    yl = d * lax.rsqrt(var + LNX_EPS) * lg_ref[...] + lb_ref[...]
    bonus = bonus_ref[...].reshape(nb * tb, rw).astype(F32)
    gate = gate_ref[...].reshape(nb * tb, rw).astype(F32)
    o_ref[...] = ((yl + bonus) * gate).astype(o_ref.dtype).reshape(nb, tb, rw)


def _rwkv_call(streams, el, consts, tb):
    b, t, rw = streams[0].shape
    npair = rw // PAIR
    names = ["lnx_g", "lnx_b", "bd", "hmask", "strict", "incl", "eye"]
    ops = [consts[nm] for nm in names]
    full = lambda arr: pl.BlockSpec(arr.shape, lambda i: (0,) * arr.ndim)
    tok = pl.BlockSpec((b, tb, rw), lambda i: (0, i, 0))
    return pl.pallas_call(
        _rwkv_kernel,
        grid=(t // tb,),
        in_specs=[tok] * len(streams)
                 + [pl.BlockSpec((b, tb // CHUNK, SUBLANES, rw), lambda i: (0, i, 0, 0))]
                 + [full(a) for a in ops],
        out_specs=tok,
        out_shape=jax.ShapeDtypeStruct((b, t, rw), BF16),
        scratch_shapes=[pltpu.VMEM((b * npair, PAIR, PAIR), F32), pltpu.VMEM((b, tb, rw), F32)]
                       + [pltpu.VMEM((b * tb // CHUNK, npair, 2 * CHUNK, PAIR), BF16)]
                       + [pltpu.VMEM((b * tb // CHUNK, npair, CHUNK, PAIR), BF16)] * 2,
        compiler_params=pltpu.CompilerParams(dimension_semantics=("arbitrary",),
                                             vmem_limit_bytes=VMEM_LIMIT),
        name="rwkv_group",
    )(*streams, el, *ops)


def _ffn_kernel(x_ref, h_ref, mod_ref, gpost_ref, wup_ref, cw_ref, cb_ref, wdn_ref,
                o_ref, z_ref, zc_ref, acc_ref):
    i = pl.program_id(1)
    tm = x_ref.shape[1]
    d_ff = wdn_ref.shape[0]
    hb = h_ref[0]

    @pl.when(i == 0)
    def _():
        zc_ref[...] = jnp.zeros_like(zc_ref)

    n_chunks = d_ff // FF_CHUNK
    cols = lambda f: (slice(f * FF_CHUNK, (f + 1) * FF_CHUNK),
                      slice(d_ff + f * FF_CHUNK, d_ff + (f + 1) * FF_CHUNK))

    def up(f):
        cg, cv = cols(f)
        return jnp.concatenate([_dot(hb, wup_ref[:, cg]), _dot(hb, wup_ref[:, cv])], axis=1)

    def conv_act(f, z):
        cg, cv = cols(f)
        zb = z_ref.at[f % 2]
        zb[0:SUBLANES, :] = zc_ref[f]
        zb[SUBLANES:, :] = z
        zc_ref[f] = z[tm - SUBLANES:, :]
        zc = jnp.concatenate([cb_ref[:, cg], cb_ref[:, cv]], axis=1)
        for k in range(FFN_K):
            o = SUBLANES - (FFN_K - 1) + k
            wk = jnp.concatenate([cw_ref[k:k + 1, cg], cw_ref[k:k + 1, cv]], axis=1)
            zc = zc + wk * zb[o:o + tm, :]
        zg = zc[:, :FF_CHUNK].astype(BF16)
        return zg * _sigmoid(zg) * zc[:, FF_CHUNK:].astype(BF16)

    act = conv_act(0, up(0))
    z_next = up(1)
    for f in range(n_chunks):
        z_cur = z_next
        if f + 2 < n_chunks:
            z_next = up(f + 2)
        contrib = _dot(act, wdn_ref[f * FF_CHUNK:(f + 1) * FF_CHUNK, :])
        if f == 0:
            acc_ref[...] = contrib
        else:
            acc_ref[...] += contrib
        if f + 1 < n_chunks:
            act = conv_act(f + 1, z_cur)
    o_ref[0] = x_ref[0] + mod_ref[0, 5:6, :] * _rms(acc_ref[...], gpost_ref[...])


def _ffn_call(x1, h2, mod, gpost, wup, cw, cb, wdn, tm):
    b, t, d = x1.shape
    full = lambda arr: pl.BlockSpec(arr.shape, lambda bi, i: (0,) * arr.ndim)
    return pl.pallas_call(
        _ffn_kernel,
        grid=(b, t // tm),
        in_specs=[pl.BlockSpec((1, tm, d), lambda bi, i: (bi, i, 0)),
                  pl.BlockSpec((1, tm, d), lambda bi, i: (bi, i, 0)),
                  pl.BlockSpec((1, 6, d), lambda bi, i: (bi, 0, 0)),
                  full(gpost), full(wup), full(cw), full(cb), full(wdn)],
        out_specs=pl.BlockSpec((1, tm, d), lambda bi, i: (bi, i, 0)),
        out_shape=jax.ShapeDtypeStruct((b, t, d), F32),
        scratch_shapes=[pltpu.VMEM((2, tm + SUBLANES, 2 * FF_CHUNK), F32),
                        pltpu.VMEM((wdn.shape[0] // FF_CHUNK, SUBLANES, 2 * FF_CHUNK), F32),
                        pltpu.VMEM((tm, d), F32)],
        compiler_params=pltpu.CompilerParams(dimension_semantics=("arbitrary", "arbitrary"),
                                             vmem_limit_bytes=VMEM_LIMIT),
        name="conv_ffn",
    )(x1, h2, mod, gpost, wup, cw, cb, wdn)


def _layer(x, mod, lw, tiles):
    b, t, d = x.shape
    cw = lw["conv_dw_w"].shape[1]
    rw = lw["w0"].shape[0]
    row = lambda vec: vec.reshape(1, -1)

    mu = row(lw["rwkv_mu"])
    ch = jnp.arange(rw) // HEAD
    rr = jnp.arange(SUB)
    pp = jnp.arange(PAIR)
    tt = jnp.arange(CHUNK)
    first = (pp < HEAD).astype(BF16)
    consts = {
        "mu": mu[:, :3 * rw], "mu_lora": mu[:, 3 * rw:],
        "w0": row(lw["w0"]), "w2": lw["w2"].astype(BF16),
        "a0": row(lw["a0"]), "a2": lw["a2"].astype(BF16),
        "g2": lw["g2"].astype(BF16),
        "k_k": row(lw["k_k"]), "k_a": row(lw["k_a"]), "r_k": row(lw["r_k"]),
        "lnx_g": row(lw["lnx_g"]), "lnx_b": row(lw["lnx_b"]),
        "bd": (ch[:, None] == ch[None, :]).astype(BF16),
        "tri": ((rr[:, None] // CHUNK == rr[None, :] // CHUNK) & (rr[None, :] <= rr[:, None])).astype(BF16),
        "hmask": jnp.broadcast_to(jnp.stack([first, 1 - first])[:, None, :], (2, CHUNK, PAIR)),
        "strict": ((pp[None, :] % CHUNK) < tt[:, None]).astype(F32),
        "incl": ((pp[None, :] % CHUNK) <= tt[:, None]).astype(F32),
        "eye": ((pp[None, :] % CHUNK) == tt[:, None]).astype(F32),
    }

    u0, *streams, el = _front_call(x, mod, row(lw["mix_pre_g"]), lw["w_in"].astype(BF16), consts, 2 * cw, tiles["inproj"])
    conv_w = jnp.broadcast_to(lw["conv_dw_w"][:, None, :], (CONV_K, SUBLANES, cw))
    y = _rwkv_call(streams, el, consts, tiles["rwkv"])
    x1, h2 = _mixout_call(u0, conv_w, row(lw["conv_dw_b"]), row(lw["conv_ln_g"]), row(lw["conv_ln_b"]), y, x,
                          mod, row(lw["mix_post_g"]), row(lw["ffn_pre_g"]), lw["w_out"].astype(BF16),
                          tiles["mixout"])
    return _ffn_call(x1, h2, mod, row(lw["ffn_post_g"]), lw["w_up"].astype(BF16), lw["ffn_dw_w"],
                     row(lw["ffn_dw_b"]), lw["w_down"].astype(BF16), tiles["ffn"])


def _tiles(t):
    pick = lambda pref: min(pref, t)
    return {"inproj": pick(512), "rwkv": pick(512), "mixout": pick(512), "ffn": pick(256)}


def kernel(x, c, ada_w, ada_b, mix_pre_g, mix_post_g, w_in, conv_dw_w, conv_dw_b, conv_ln_g,
           conv_ln_b, rwkv_mu, w0, w2, a0, a2, g2, k_k, k_a, r_k, lnx_g, lnx_b, w_out, ffn_pre_g,
           ffn_post_g, w_up, ffn_dw_w, ffn_dw_b, w_down):
    b, t, d = x.shape
    depth = ada_w.shape[0]
    tiles = _tiles(t)
    c8 = jnp.pad(c, ((0, 8 - b), (0, 0))).T
    for l in range(depth):
        mod = _mod_call(c8, ada_w[l], ada_b[l].reshape(1, -1), 128, b)[:b].reshape(b, 6, d)
        lw = {"mix_pre_g": mix_pre_g[l], "mix_post_g": mix_post_g[l], "w_in": w_in[l],
              "conv_dw_w": conv_dw_w[l], "conv_dw_b": conv_dw_b[l], "conv_ln_g": conv_ln_g[l],
              "conv_ln_b": conv_ln_b[l], "rwkv_mu": rwkv_mu[l], "w0": w0[l], "w2": w2[l],
              "a0": a0[l], "a2": a2[l], "g2": g2[l], "k_k": k_k[l], "k_a": k_a[l],
              "r_k": r_k[l].reshape(-1), "lnx_g": lnx_g[l], "lnx_b": lnx_b[l], "w_out": w_out[l],
              "ffn_pre_g": ffn_pre_g[l], "ffn_post_g": ffn_post_g[l], "w_up": w_up[l],
              "ffn_dw_w": ffn_dw_w[l], "ffn_dw_b": ffn_dw_b[l], "w_down": w_down[l]}
        x = _layer(x, mod, lw, tiles)
    return x
```

```python
import functools
import math

import jax
import jax.numpy as jnp
from jax import lax
from jax.experimental import pallas as pl
from jax.experimental.pallas import tpu as pltpu

F32 = jnp.float32
BF16 = jnp.bfloat16

HEAD = 64
CHUNK = 64
PAIR = 2 * HEAD
CONV_K = 31
FFN_K = 3
RMS_EPS = 1e-6
LN_EPS = 1e-5
LNX_EPS = 64e-5
SUBLANES = 8
CONV_HALO = 32
FF_CHUNK = 256
A_GROUP = 4
SUB = 256
VMEM_LIMIT = 56 * 1024 * 1024

NT = (((1,), (1,)), ((), ()))
TN = (((0,), (0,)), ((), ()))


def _dot(a, b):
    return jnp.dot(a, b, preferred_element_type=F32)


def _dg(a, b, dims):
    return lax.dot_general(a, b, dims, preferred_element_type=F32)


def _sigmoid(x):
    return jax.nn.sigmoid(x)


def _rms(x, g):
    ms = jnp.mean(x * x, axis=-1, keepdims=True)
    return x * lax.rsqrt(ms + RMS_EPS) * g


def _mod_kernel(ct_ref, w_ref, b_ref, o_ref, *, nb):
    k = pl.program_id(0)
    cs = ct_ref[...]
    s = cs * _sigmoid(cs)
    w = w_ref[...]

    @pl.when(k == 0)
    def _():
        o_ref[...] = jnp.broadcast_to(b_ref[...], o_ref.shape)

    for bi in range(nb):
        o_ref[bi:bi + 1, :] += jnp.sum(w * s[:, bi:bi + 1], axis=0, keepdims=True)


def _mod_call(c8, ada_w, ada_b, tk, nb):
    d, n = ada_w.shape
    return pl.pallas_call(
        functools.partial(_mod_kernel, nb=nb),
        grid=(d // tk,),
        in_specs=[pl.BlockSpec((tk, 8), lambda k: (k, 0)),
                  pl.BlockSpec((tk, n), lambda k: (k, 0)),
                  pl.BlockSpec((1, n), lambda k: (0, 0))],
        out_specs=pl.BlockSpec((8, n), lambda k: (0, 0)),
        out_shape=jax.ShapeDtypeStruct((8, n), F32),
        compiler_params=pltpu.CompilerParams(dimension_semantics=("arbitrary",),
                                             vmem_limit_bytes=VMEM_LIMIT),
        name="adaln_mod",
    )(c8, ada_w, ada_b)


def _front_kernel(x_ref, mod_ref, g_ref, w_ref, mu_ref, mul_ref, w0_ref, w2_ref, a0_ref, a2_ref, g2_ref,
                  kk_ref, ka_ref, rk_ref, bd_ref, tri_ref,
                  u0_ref, kt_ref, rt_ref, kh_ref, bh_ref, v_ref, bonus_ref, gate_ref, el_ref,
                  prev_ref, prevl_ref):
    i = pl.program_id(1)
    tm = x_ref.shape[1]
    cw = u0_ref.shape[2]
    nc = 2 * cw
    rw = v_ref.shape[2]

    @pl.when(i == 0)
    def _():
        prev_ref[...] = jnp.zeros_like(prev_ref)
        prevl_ref[...] = jnp.zeros_like(prevl_ref)

    bd = bd_ref[...]
    tri = tri_ref[...]
    rows = lax.broadcasted_iota(jnp.int32, (SUB, 1), 0)

    def shifted(p, prev_row, mu):
        prev = jnp.where(rows == 0, prev_row[0:1, :], pltpu.roll(p, 1, axis=0))
        prev_row[0:1, :] = p[SUB - 1:SUB, :]
        return p + (prev - p) * mu

    def prep(p, pl_, r0):
        xs = shifted(p, prev_ref, mu_ref[...])
        xl = shifted(pl_, prevl_ref, mul_ref[...])
        r = xs[:, 0:rw]
        k = xs[:, rw:2 * rw]
        v = xs[:, 2 * rw:3 * rw]
        dl = w2_ref.shape[0]
        al = a2_ref.shape[0]
        wd = xl[:, 0:dl]
        ad = xl[:, dl:dl + al]
        gd = xl[:, dl + al:]
        zw = w0_ref[...] + _dot(jnp.tanh(wd).astype(BF16), w2_ref[...])
        lw = -math.exp(-0.5) * _sigmoid(zw)
        a = _sigmoid(a0_ref[...] + _dot(ad.astype(BF16), a2_ref[...]))
        gate = _dot(_sigmoid(gd).astype(BF16), g2_ref[...])
        kkr = k * kk_ref[...]
        ss = _dot((kkr * kkr).astype(BF16), bd)
        kk = kkr * lax.rsqrt(jnp.maximum(ss, 1e-24))
        km = k * (1.0 + (a - 1.0) * ka_ref[...])
        bonus = _dot((r * km * rk_ref[...]).astype(BF16), bd) * v
        hi = lw.astype(BF16)
        lo = (lw - hi.astype(F32)).astype(BF16)
        c = _dot(tri, hi) + _dot(tri, lo)
        ec = jnp.exp(c)
        ecn = jnp.exp(-c)
        rs = slice(r0, r0 + SUB)
        kt_ref[0, rs, :] = (kk * jnp.exp(c - lw)).astype(BF16)
        rt_ref[0, rs, :] = (r * ec).astype(BF16)
        kh_ref[0, rs, :] = (km * ecn).astype(BF16)
        bh_ref[0, rs, :] = (a * kk * ecn).astype(BF16)
        v_ref[0, rs, :] = v.astype(BF16)
        bonus_ref[0, rs, :] = bonus.astype(BF16)
        gate_ref[0, rs, :] = gate.astype(BF16)
        for j in range(SUB // CHUNK):
            last = ec[(j + 1) * CHUNK - 1:(j + 1) * CHUNK, :]
            el_ref[0, r0 // CHUNK + j] = jnp.broadcast_to(last, (SUBLANES, rw))

    n_main = nc + 3 * rw

    def proj(r0):
        x = x_ref[0, r0:r0 + SUB, :]
        h = (_rms(x, g_ref[...]) * (1.0 + mod_ref[0, 1:2, :]) + mod_ref[0, 0:1, :]).astype(BF16)
        return _dot(h, w_ref[:, 0:n_main]), _dot(h, w_ref[:, n_main:])

    p_next = proj(0)
    for r0 in range(0, tm, SUB):
        p = p_next
        if r0 + SUB < tm:
            p_next = proj(r0 + SUB)
        u0_ref[0, r0:r0 + SUB, :] = (p[0][:, :cw] * _sigmoid(p[0][:, cw:nc])).astype(BF16)
        prep(p[0][:, nc:], p[1], r0)


def _front_call(x, mod, g, w, consts, n_conv, tm):
    b, t, d = x.shape
    n = w.shape[1]
    rw = consts["w0"].shape[1]
    names = ["mu", "mu_lora", "w0", "w2", "a0", "a2", "g2", "k_k", "k_a", "r_k", "bd", "tri"]
    ops = [consts[nm] for nm in names]
    full = lambda arr: pl.BlockSpec(arr.shape, lambda bi, i: (0,) * arr.ndim)
    tok = lambda width: pl.BlockSpec((1, tm, width), lambda bi, i: (bi, i, 0))
    tok_shape = lambda width, dt: jax.ShapeDtypeStruct((b, t, width), dt)
    return pl.pallas_call(
        _front_kernel,
        grid=(b, t // tm),
        in_specs=[tok(d), pl.BlockSpec((1, 6, d), lambda bi, i: (bi, 0, 0)),
                  pl.BlockSpec((1, d), lambda bi, i: (0, 0)),
                  pl.BlockSpec((d, n), lambda bi, i: (0, 0))] + [full(a) for a in ops],
        out_specs=[tok(n_conv // 2)] + [tok(rw)] * 7
                  + [pl.BlockSpec((1, tm // CHUNK, SUBLANES, rw), lambda bi, i: (bi, i, 0, 0))],
        out_shape=[tok_shape(n_conv // 2, BF16)] + [tok_shape(rw, BF16)] * 7
                  + [jax.ShapeDtypeStruct((b, t // CHUNK, SUBLANES, rw), F32)],
        scratch_shapes=[pltpu.VMEM((SUBLANES, 3 * rw), F32),
                        pltpu.VMEM((SUBLANES, n - n_conv - 3 * rw), F32)],
        compiler_params=pltpu.CompilerParams(dimension_semantics=("arbitrary", "arbitrary"),
                                             vmem_limit_bytes=VMEM_LIMIT),
        name="inproj_prep",
    )(x, mod, g, w, *ops)


def _mixout_kernel(u0_ref, halo_ref, cwt_ref, cb_ref, lg_ref, lb_ref, y_ref, x_ref, mod_ref, g_ref, gf_ref,
                   w_ref, o_ref, h_ref, scr, shifted, u_scr, my_scr, *, rb):
    i = pl.program_id(1)
    tm = u0_ref.shape[1]
    cw = y_ref.shape[2]
    my_scr[...] = _dot(y_ref[0], w_ref[cw:, :])

    scr[CONV_HALO:CONV_HALO + tm, :] = u0_ref[0].astype(F32)
    scr[0:CONV_HALO, :] = jnp.where(i > 0, halo_ref[0].astype(F32), 0.0)
    span = tm + CONV_HALO - SUBLANES
    for r in range(1, SUBLANES):
        shifted[r - 1, 0:span, :] = scr[r:r + span, :]
    off = CONV_HALO - (CONV_K - 1)
    for r0 in range(0, tm, rb):
        acc = jnp.zeros((rb // SUBLANES, SUBLANES, cw), F32) + cb_ref[...]
        for k in range(CONV_K):
            r = (off + k) % SUBLANES
            a0 = r0 + off + k - r
            src = scr[a0:a0 + rb, :] if r == 0 else shifted[r - 1, a0:a0 + rb, :]
            acc = acc + cwt_ref[k] * src.reshape(rb // SUBLANES, SUBLANES, cw)
        acc = acc.reshape(rb, cw)
        mu = jnp.mean(acc, axis=-1, keepdims=True)
        dd = acc - mu
        var = jnp.mean(dd * dd, axis=-1, keepdims=True)
        z = dd * lax.rsqrt(var + LN_EPS) * lg_ref[...] + lb_ref[...]
        u_scr[r0:r0 + rb, :] = (z * _sigmoid(z)).astype(BF16)

    m = _dot(u_scr[...], w_ref[0:cw, :]) + my_scr[...]
    x1 = x_ref[0] + mod_ref[0, 2:3, :] * _rms(m, g_ref[...])
    o_ref[0] = x1
    h_ref[0] = (_rms(x1, gf_ref[...]) * (1.0 + mod_ref[0, 4:5, :]) + mod_ref[0, 3:4, :]).astype(h_ref.dtype)


def _mixout_call(u0, conv_w, conv_b, lg, lb, y, x, mod, g, gf, w, tm, rb=16):
    b, t, d = x.shape
    cw = u0.shape[2]
    hb = tm // CONV_HALO
    row_blk = lambda width: pl.BlockSpec((1, tm, width), lambda bi, i: (bi, i, 0))
    vec = lambda width: pl.BlockSpec((1, width), lambda bi, i: (0, 0))
    return pl.pallas_call(
        functools.partial(_mixout_kernel, rb=rb),
        grid=(b, t // tm),
        in_specs=[row_blk(cw),
                  pl.BlockSpec((1, CONV_HALO, cw), lambda bi, i: (bi, jnp.maximum(i * hb - 1, 0), 0)),
                  pl.BlockSpec((CONV_K, SUBLANES, cw), lambda bi, i: (0, 0, 0)),
                  vec(cw), vec(cw), vec(cw),
                  row_blk(y.shape[2]), row_blk(d),
                  pl.BlockSpec((1, 6, d), lambda bi, i: (bi, 0, 0)),
                  vec(d), vec(d),
                  pl.BlockSpec(w.shape, lambda bi, i: (0, 0))],
        out_specs=[row_blk(d), row_blk(d)],
        out_shape=[jax.ShapeDtypeStruct((b, t, d), F32), jax.ShapeDtypeStruct((b, t, d), BF16)],
        scratch_shapes=[pltpu.VMEM((tm + CONV_HALO, cw), F32),
                        pltpu.VMEM((SUBLANES - 1, tm + CONV_HALO - SUBLANES, cw), F32),
                        pltpu.VMEM((tm, cw), BF16),
                        pltpu.VMEM((tm, d), F32)],
        compiler_params=pltpu.CompilerParams(dimension_semantics=("arbitrary", "arbitrary"),
                                             vmem_limit_bytes=VMEM_LIMIT),
        name="conv_outproj",
    )(u0, u0, conv_w, conv_b, lg, lb, y, x, mod, g, gf, w)


def _stack(val, m0, m1):
    return jnp.concatenate([val * m0, val * m1], axis=0)


def _neumann_inverse(n_w, eye_w, m0, m1):
    levels = CHUNK.bit_length() - 2
    t_w = [eye_w + n_ for n_ in n_w]
    pb = [n_.astype(BF16) for n_ in n_w]
    pk = [_dot(x, _stack(x, m0, m1)) for x in pb]
    for k in range(1, levels + 1):
        pkb = [x.astype(BF16) for x in pk]
        pd = [_stack(x, m0, m1) for x in pkb]
        if k < levels:
            both = [_dot(jnp.concatenate([t_.astype(BF16), p_], axis=0), d_)
                    for t_, p_, d_ in zip(t_w, pkb, pd)]
            t_w = [t_ + b_[:CHUNK] for t_, b_ in zip(t_w, both)]
            pk = [b_[CHUNK:] for b_ in both]
        else:
            t_w = [t_ + _dot(t_.astype(BF16), d_) for t_, d_ in zip(t_w, pd)]
    return t_w


def _rwkv_kernel(kt_ref, rt_ref, kh_ref, bh_ref, v_ref, bonus_ref, gate_ref, el_ref,
                 lg_ref, lb_ref, bd_ref, hmask_ref, strict_ref, incl_ref, eye_ref,
                 o_ref,
                 s_ref, y_ref, ak_ref, abr_ref, t_ref):
    i = pl.program_id(0)
    nb, tb, rw = o_ref.shape
    npair = rw // PAIR
    cpb = tb // CHUNK

    @pl.when(i == 0)
    def _():
        s_ref[...] = jnp.zeros_like(s_ref)

    strict = strict_ref[...]
    incl = incl_ref[...]
    eye = eye_ref[...]
    m0 = hmask_ref[0]
    m1 = hmask_ref[1]
    pairs = range(npair)
    lanes = [slice(pi * PAIR, (pi + 1) * PAIR) for pi in pairs]
    stack = lambda val: _stack(val, m0, m1)

    def rows_of(j):
        return pl.ds(pl.multiple_of(j * CHUNK, CHUNK), CHUNK)

    def phase_a(gi, carry):
        inst = [(bi, gi * A_GROUP + j, pi) for bi in range(nb) for j in range(A_GROUP) for pi in pairs]
        sl = [(bi, rows_of(j), lanes[pi]) for bi, j, pi in inst]
        lhs = [jnp.concatenate([kt_ref[x], rt_ref[x]], axis=0) for x in sl]
        kb = [jnp.concatenate([stack(kh_ref[x]), stack(bh_ref[x])], axis=0) for x in sl]
        g = [_dg(l_, k_, NT) for l_, k_ in zip(lhs, kb)]
        for g_, (bi, j, pi) in zip(g, inst):
            ci = bi * cpb + j
            ak_ref[ci, pi, 0:CHUNK] = (g_[0:CHUNK, 0:PAIR] * strict).astype(BF16)
            ak_ref[ci, pi, CHUNK:] = (g_[CHUNK:, 0:PAIR] * incl).astype(BF16)
            abr_ref[ci, pi] = (g_[CHUNK:, PAIR:] * incl).astype(BF16)
        t_w = _neumann_inverse([-(g_[0:CHUNK, PAIR:] * strict) for g_ in g], eye, m0, m1)
        for t_, (bi, j, pi) in zip(t_w, inst):
            t_ref[bi * cpb + j, pi] = t_.astype(BF16)
        return carry

    lax.fori_loop(0, cpb // A_GROUP, phase_a, 0)

    def phase_b(j, carry):
        chains = [(bi, pi) for bi in range(nb) for pi in pairs]
        cis = [bi * cpb + j for bi, _ in chains]
        rs = rows_of(j)
        sl = [(bi, rs, lanes[pi]) for bi, pi in chains]
        lhs = [jnp.concatenate([kt_ref[x], rt_ref[x]], axis=0) for x in sl]
        kb = [jnp.concatenate([stack(kh_ref[x]), -stack(bh_ref[x])], axis=0) for x in sl]
        vs = [stack(v_ref[x]) for x in sl]
        s = [s_ref[bi * npair + pi] for bi, pi in chains]
        sb = [x.astype(BF16) for x in s]
        xy = [_dg(l_, sb_, NT) + _dot(ak_ref[ci, pi], v_)
              for l_, sb_, v_, ci, (_, pi) in zip(lhs, sb, vs, cis, chains)]
        us = [stack(_dot(t_ref[ci, pi], stack(x_[0:CHUNK].astype(BF16))).astype(BF16))
              for x_, ci, (_, pi) in zip(xy, cis, chains)]
        ds_ = [_dg(jnp.concatenate([v_, u_], axis=0), k_, TN) for v_, u_, k_ in zip(vs, us, kb)]
        for s_, d_, (bi, pi) in zip(s, ds_, chains):
            e_l = el_ref[bi, j][0:1, lanes[pi]]
            s_ref[bi * npair + pi] = (s_ + d_) * e_l
        for x_, u_, x, ci, (_, pi) in zip(xy, us, sl, cis, chains):
            y_ref[x] = x_[CHUNK:] - _dot(abr_ref[ci, pi], u_)
        return carry

    lax.fori_loop(0, cpb, phase_b, 0, unroll=4)

    bd = bd_ref[...]
    y = y_ref[...].reshape(nb * tb, rw)
    inv = 1.0 / HEAD
    mean = _dot(y.astype(BF16), bd) * inv
    d = y - mean
    var = _dot((d * d).astype(BF16), bd) * inv
    yl = d * lax.rsqrt(var + LNX_EPS) * lg_ref[...] + lb_ref[...]
    bonus = bonus_ref[...].reshape(nb * tb, rw).astype(F32)
    gate = gate_ref[...].reshape(nb * tb, rw).astype(F32)
    o_ref[...] = ((yl + bonus) * gate).astype(o_ref.dtype).reshape(nb, tb, rw)


def _rwkv_call(streams, el, consts, tb):
    b, t, rw = streams[0].shape
    npair = rw // PAIR
    names = ["lnx_g", "lnx_b", "bd", "hmask", "strict", "incl", "eye"]
    ops = [consts[nm] for nm in names]
    full = lambda arr: pl.BlockSpec(arr.shape, lambda i: (0,) * arr.ndim)
    tok = pl.BlockSpec((b, tb, rw), lambda i: (0, i, 0))
    return pl.pallas_call(
        _rwkv_kernel,
        grid=(t // tb,),
        in_specs=[tok] * len(streams)
                 + [pl.BlockSpec((b, tb // CHUNK, SUBLANES, rw), lambda i: (0, i, 0, 0))]
                 + [full(a) for a in ops],
        out_specs=tok,
        out_shape=jax.ShapeDtypeStruct((b, t, rw), BF16),
        scratch_shapes=[pltpu.VMEM((b * npair, PAIR, PAIR), F32), pltpu.VMEM((b, tb, rw), F32)]
                       + [pltpu.VMEM((b * tb // CHUNK, npair, 2 * CHUNK, PAIR), BF16)]
                       + [pltpu.VMEM((b * tb // CHUNK, npair, CHUNK, PAIR), BF16)] * 2,
        compiler_params=pltpu.CompilerParams(dimension_semantics=("arbitrary",),
                                             vmem_limit_bytes=VMEM_LIMIT),
        name="rwkv_group",
    )(*streams, el, *ops)


def _ffn_kernel(x_ref, h_ref, mod_ref, gpost_ref, wup_ref, cw_ref, cb_ref, wdn_ref,
                o_ref, z_ref, zc_ref, acc_ref):
    i = pl.program_id(1)
    tm = x_ref.shape[1]
    d_ff = wdn_ref.shape[0]
    hb = h_ref[0]

    @pl.when(i == 0)
    def _():
        zc_ref[...] = jnp.zeros_like(zc_ref)

    n_chunks = d_ff // FF_CHUNK
    cols = lambda f: (slice(f * FF_CHUNK, (f + 1) * FF_CHUNK),
                      slice(d_ff + f * FF_CHUNK, d_ff + (f + 1) * FF_CHUNK))

    def up(f):
        cg, cv = cols(f)
        return jnp.concatenate([_dot(hb, wup_ref[:, cg]), _dot(hb, wup_ref[:, cv])], axis=1)

    def conv_act(f, z):
        cg, cv = cols(f)
        zb = z_ref.at[f % 2]
        zb[0:SUBLANES, :] = zc_ref[f]
        zb[SUBLANES:, :] = z
        zc_ref[f] = z[tm - SUBLANES:, :]
        zc = jnp.concatenate([cb_ref[:, cg], cb_ref[:, cv]], axis=1)
        for k in range(FFN_K):
            o = SUBLANES - (FFN_K - 1) + k
            wk = jnp.concatenate([cw_ref[k:k + 1, cg], cw_ref[k:k + 1, cv]], axis=1)
            zc = zc + wk * zb[o:o + tm, :]
        zg = zc[:, :FF_CHUNK].astype(BF16)
        return zg * _sigmoid(zg) * zc[:, FF_CHUNK:].astype(BF16)

    act = conv_act(0, up(0))
    z_next = up(1)
    for f in range(n_chunks):
        z_cur = z_next
        if f + 2 < n_chunks:
            z_next = up(f + 2)
        contrib = _dot(act, wdn_ref[f * FF_CHUNK:(f + 1) * FF_CHUNK, :])
        if f == 0:
            acc_ref[...] = contrib
        else:
            acc_ref[...] += contrib
        if f + 1 < n_chunks:
            act = conv_act(f + 1, z_cur)
    o_ref[0] = x_ref[0] + mod_ref[0, 5:6, :] * _rms(acc_ref[...], gpost_ref[...])


def _ffn_call(x1, h2, mod, gpost, wup, cw, cb, wdn, tm):
    b, t, d = x1.shape
    full = lambda arr: pl.BlockSpec(arr.shape, lambda bi, i: (0,) * arr.ndim)
    return pl.pallas_call(
        _ffn_kernel,
        grid=(b, t // tm),
        in_specs=[pl.BlockSpec((1, tm, d), lambda bi, i: (bi, i, 0)),
                  pl.BlockSpec((1, tm, d), lambda bi, i: (bi, i, 0)),
                  pl.BlockSpec((1, 6, d), lambda bi, i: (bi, 0, 0)),
                  full(gpost), full(wup), full(cw), full(cb), full(wdn)],
        out_specs=pl.BlockSpec((1, tm, d), lambda bi, i: (bi, i, 0)),
        out_shape=jax.ShapeDtypeStruct((b, t, d), F32),
        scratch_shapes=[pltpu.VMEM((2, tm + SUBLANES, 2 * FF_CHUNK), F32),
                        pltpu.VMEM((wdn.shape[0] // FF_CHUNK, SUBLANES, 2 * FF_CHUNK), F32),
                        pltpu.VMEM((tm, d), F32)],
        compiler_params=pltpu.CompilerParams(dimension_semantics=("arbitrary", "arbitrary"),
                                             vmem_limit_bytes=VMEM_LIMIT),
        name="conv_ffn",
    )(x1, h2, mod, gpost, wup, cw, cb, wdn)


def _layer(x, mod, lw, tiles):
    b, t, d = x.shape
    cw = lw["conv_dw_w"].shape[1]
    rw = lw["w0"].shape[0]
    row = lambda vec: vec.reshape(1, -1)

    mu = row(lw["rwkv_mu"])
    ch = jnp.arange(rw) // HEAD
    rr = jnp.arange(SUB)
    pp = jnp.arange(PAIR)
    tt = jnp.arange(CHUNK)
    first = (pp < HEAD).astype(BF16)
    consts = {
        "mu": mu[:, :3 * rw], "mu_lora": mu[:, 3 * rw:],
        "w0": row(lw["w0"]), "w2": lw["w2"].astype(BF16),
        "a0": row(lw["a0"]), "a2": lw["a2"].astype(BF16),
        "g2": lw["g2"].astype(BF16),
        "k_k": row(lw["k_k"]), "k_a": row(lw["k_a"]), "r_k": row(lw["r_k"]),
        "lnx_g": row(lw["lnx_g"]), "lnx_b": row(lw["lnx_b"]),
        "bd": (ch[:, None] == ch[None, :]).astype(BF16),
        "tri": ((rr[:, None] // CHUNK == rr[None, :] // CHUNK) & (rr[None, :] <= rr[:, None])).astype(BF16),
        "hmask": jnp.broadcast_to(jnp.stack([first, 1 - first])[:, None, :], (2, CHUNK, PAIR)),
        "strict": ((pp[None, :] % CHUNK) < tt[:, None]).astype(F32),
        "incl": ((pp[None, :] % CHUNK) <= tt[:, None]).astype(F32),
        "eye": ((pp[None, :] % CHUNK) == tt[:, None]).astype(F32),
    }

    u0, *streams, el = _front_call(x, mod, row(lw["mix_pre_g"]), lw["w_in"].astype(BF16), consts, 2 * cw, tiles["inproj"])
    conv_w = jnp.broadcast_to(lw["conv_dw_w"][:, None, :], (CONV_K, SUBLANES, cw))
    y = _rwkv_call(streams, el, consts, tiles["rwkv"])
    x1, h2 = _mixout_call(u0, conv_w, row(lw["conv_dw_b"]), row(lw["conv_ln_g"]), row(lw["conv_ln_b"]), y, x,
                          mod, row(lw["mix_post_g"]), row(lw["ffn_pre_g"]), lw["w_out"].astype(BF16),
                          tiles["mixout"])
    return _ffn_call(x1, h2, mod, row(lw["ffn_post_g"]), lw["w_up"].astype(BF16), lw["ffn_dw_w"],
                     row(lw["ffn_dw_b"]), lw["w_down"].astype(BF16), tiles["ffn"])


def _tiles(t):
    pick = lambda pref: min(pref, t)
    return {"inproj": pick(512), "rwkv": pick(512), "mixout": pick(512), "ffn": pick(256)}


def kernel(x, c, ada_w, ada_b, mix_pre_g, mix_post_g, w_in, conv_dw_w, conv_dw_b, conv_ln_g,
           conv_ln_b, rwkv_mu, w0, w2, a0, a2, g2, k_k, k_a, r_k, lnx_g, lnx_b, w_out, ffn_pre_g,
           ffn_post_g, w_up, ffn_dw_w, ffn_dw_b, w_down):
    b, t, d = x.shape
    depth = ada_w.shape[0]
    tiles = _tiles(t)
    c8 = jnp.pad(c, ((0, 8 - b), (0, 0))).T
    for l in range(depth):
        mod = _mod_call(c8, ada_w[l], ada_b[l].reshape(1, -1), 256, b)[:b].reshape(b, 6, d)
        lw = {"mix_pre_g": mix_pre_g[l], "mix_post_g": mix_post_g[l], "w_in": w_in[l],
              "conv_dw_w": conv_dw_w[l], "conv_dw_b": conv_dw_b[l], "conv_ln_g": conv_ln_g[l],
              "conv_ln_b": conv_ln_b[l], "rwkv_mu": rwkv_mu[l], "w0": w0[l], "w2": w2[l],
              "a0": a0[l], "a2": a2[l], "g2": g2[l], "k_k": k_k[l], "k_a": k_a[l],
              "r_k": r_k[l].reshape(-1), "lnx_g": lnx_g[l], "lnx_b": lnx_b[l], "w_out": w_out[l],
              "ffn_pre_g": ffn_pre_g[l], "ffn_post_g": ffn_post_g[l], "w_up": w_up[l],
              "ffn_dw_w": ffn_dw_w[l], "ffn_dw_b": ffn_dw_b[l], "w_down": w_down[l]}
        x = _layer(x, mod, lw, tiles)
    return x
```

```python
import functools
import math

import jax
import jax.numpy as jnp
from jax import lax
from jax.experimental import pallas as pl
from jax.experimental.pallas import tpu as pltpu

F32 = jnp.float32
BF16 = jnp.bfloat16

HEAD = 64
CHUNK = 64
PAIR = 2 * HEAD
CONV_K = 31
FFN_K = 3
RMS_EPS = 1e-6
LN_EPS = 1e-5
LNX_EPS = 64e-5
SUBLANES = 8
CONV_HALO = 32
FF_CHUNK = 256
A_GROUP = 8
SUB = 256
VMEM_LIMIT = 56 * 1024 * 1024

NT = (((1,), (1,)), ((), ()))
TN = (((0,), (0,)), ((), ()))


def _dot(a, b):
    return jnp.dot(a, b, preferred_element_type=F32)


def _dg(a, b, dims):
    return lax.dot_general(a, b, dims, preferred_element_type=F32)


def _sigmoid(x):
    return jax.nn.sigmoid(x)


def _rms(x, g):
    ms = jnp.mean(x * x, axis=-1, keepdims=True)
    return x * lax.rsqrt(ms + RMS_EPS) * g


def _mod_kernel(ct_ref, w_ref, b_ref, o_ref, *, nb):
    k = pl.program_id(0)
    cs = ct_ref[...]
    s = cs * _sigmoid(cs)
    w = w_ref[...]

    @pl.when(k == 0)
    def _():
        o_ref[...] = jnp.broadcast_to(b_ref[...], o_ref.shape)

    for bi in range(nb):
        o_ref[bi:bi + 1, :] += jnp.sum(w * s[:, bi:bi + 1], axis=0, keepdims=True)


def _mod_call(c8, ada_w, ada_b, tk, nb):
    d, n = ada_w.shape
    return pl.pallas_call(
        functools.partial(_mod_kernel, nb=nb),
        grid=(d // tk,),
        in_specs=[pl.BlockSpec((tk, 8), lambda k: (k, 0)),
                  pl.BlockSpec((tk, n), lambda k: (k, 0)),
                  pl.BlockSpec((1, n), lambda k: (0, 0))],
        out_specs=pl.BlockSpec((8, n), lambda k: (0, 0)),
        out_shape=jax.ShapeDtypeStruct((8, n), F32),
        compiler_params=pltpu.CompilerParams(dimension_semantics=("arbitrary",),
                                             vmem_limit_bytes=VMEM_LIMIT),
        name="adaln_mod",
    )(c8, ada_w, ada_b)


def _front_kernel(x_ref, mod_ref, g_ref, w_ref, mu_ref, mul_ref, w0_ref, w2_ref, a0_ref, a2_ref, g2_ref,
                  kk_ref, ka_ref, rk_ref, bd_ref, tri_ref,
                  u0_ref, kt_ref, rt_ref, kh_ref, bh_ref, v_ref, bonus_ref, gate_ref, el_ref,
                  prev_ref, prevl_ref):
    i = pl.program_id(1)
    tm = x_ref.shape[1]
    cw = u0_ref.shape[2]
    nc = 2 * cw
    rw = v_ref.shape[2]

    @pl.when(i == 0)
    def _():
        prev_ref[...] = jnp.zeros_like(prev_ref)
        prevl_ref[...] = jnp.zeros_like(prevl_ref)

    bd = bd_ref[...]
    tri = tri_ref[...]
    rows = lax.broadcasted_iota(jnp.int32, (SUB, 1), 0)

    def shifted(p, prev_row, mu):
        prev = jnp.where(rows == 0, prev_row[0:1, :], pltpu.roll(p, 1, axis=0))
        prev_row[0:1, :] = p[SUB - 1:SUB, :]
        return p + (prev - p) * mu

    def prep(p, pl_, r0):
        xs = shifted(p, prev_ref, mu_ref[...])
        xl = shifted(pl_, prevl_ref, mul_ref[...])
        r = xs[:, 0:rw]
        k = xs[:, rw:2 * rw]
        v = xs[:, 2 * rw:3 * rw]
        dl = w2_ref.shape[0]
        al = a2_ref.shape[0]
        wd = xl[:, 0:dl]
        ad = xl[:, dl:dl + al]
        gd = xl[:, dl + al:]
        zw = w0_ref[...] + _dot(jnp.tanh(wd).astype(BF16), w2_ref[...])
        lw = -math.exp(-0.5) * _sigmoid(zw)
        a = _sigmoid(a0_ref[...] + _dot(ad.astype(BF16), a2_ref[...]))
        gate = _dot(_sigmoid(gd).astype(BF16), g2_ref[...])
        kkr = k * kk_ref[...]
        ss = _dot((kkr * kkr).astype(BF16), bd)
        kk = kkr * lax.rsqrt(jnp.maximum(ss, 1e-24))
        km = k * (1.0 + (a - 1.0) * ka_ref[...])
        bonus = _dot((r * km * rk_ref[...]).astype(BF16), bd) * v
        hi = lw.astype(BF16)
        lo = (lw - hi.astype(F32)).astype(BF16)
        c = _dot(tri, hi) + _dot(tri, lo)
        ec = jnp.exp(c)
        ecn = jnp.exp(-c)
        rs = slice(r0, r0 + SUB)
        kt_ref[0, rs, :] = (kk * jnp.exp(c - lw)).astype(BF16)
        rt_ref[0, rs, :] = (r * ec).astype(BF16)
        kh_ref[0, rs, :] = (km * ecn).astype(BF16)
        bh_ref[0, rs, :] = (a * kk * ecn).astype(BF16)
        v_ref[0, rs, :] = v.astype(BF16)
        bonus_ref[0, rs, :] = bonus.astype(BF16)
        gate_ref[0, rs, :] = gate.astype(BF16)
        for j in range(SUB // CHUNK):
            last = ec[(j + 1) * CHUNK - 1:(j + 1) * CHUNK, :]
            el_ref[0, r0 // CHUNK + j] = jnp.broadcast_to(last, (SUBLANES, rw))

    n_main = nc + 3 * rw

    def proj(r0):
        x = x_ref[0, r0:r0 + SUB, :]
        h = (_rms(x, g_ref[...]) * (1.0 + mod_ref[0, 1:2, :]) + mod_ref[0, 0:1, :]).astype(BF16)
        return _dot(h, w_ref[:, 0:n_main]), _dot(h, w_ref[:, n_main:])

    p_next = proj(0)
    for r0 in range(0, tm, SUB):
        p = p_next
        if r0 + SUB < tm:
            p_next = proj(r0 + SUB)
        u0_ref[0, r0:r0 + SUB, :] = (p[0][:, :cw] * _sigmoid(p[0][:, cw:nc])).astype(BF16)
        prep(p[0][:, nc:], p[1], r0)


def _front_call(x, mod, g, w, consts, n_conv, tm):
    b, t, d = x.shape
    n = w.shape[1]
    rw = consts["w0"].shape[1]
    names = ["mu", "mu_lora", "w0", "w2", "a0", "a2", "g2", "k_k", "k_a", "r_k", "bd", "tri"]
    ops = [consts[nm] for nm in names]
    full = lambda arr: pl.BlockSpec(arr.shape, lambda bi, i: (0,) * arr.ndim)
    tok = lambda width: pl.BlockSpec((1, tm, width), lambda bi, i: (bi, i, 0))
    tok_shape = lambda width, dt: jax.ShapeDtypeStruct((b, t, width), dt)
    return pl.pallas_call(
        _front_kernel,
        grid=(b, t // tm),
        in_specs=[tok(d), pl.BlockSpec((1, 6, d), lambda bi, i: (bi, 0, 0)),
                  pl.BlockSpec((1, d), lambda bi, i: (0, 0)),
                  pl.BlockSpec((d, n), lambda bi, i: (0, 0))] + [full(a) for a in ops],
        out_specs=[tok(n_conv // 2)] + [tok(rw)] * 7
                  + [pl.BlockSpec((1, tm // CHUNK, SUBLANES, rw), lambda bi, i: (bi, i, 0, 0))],
        out_shape=[tok_shape(n_conv // 2, BF16)] + [tok_shape(rw, BF16)] * 7
                  + [jax.ShapeDtypeStruct((b, t // CHUNK, SUBLANES, rw), F32)],
        scratch_shapes=[pltpu.VMEM((SUBLANES, 3 * rw), F32),
                        pltpu.VMEM((SUBLANES, n - n_conv - 3 * rw), F32)],
        compiler_params=pltpu.CompilerParams(dimension_semantics=("arbitrary", "arbitrary"),
                                             vmem_limit_bytes=VMEM_LIMIT),
        name="inproj_prep",
    )(x, mod, g, w, *ops)


def _mixout_kernel(u0_ref, halo_ref, cwt_ref, cb_ref, lg_ref, lb_ref, y_ref, x_ref, mod_ref, g_ref, gf_ref,
                   w_ref, o_ref, h_ref, scr, shifted, u_scr, my_scr, *, rb):
    i = pl.program_id(1)
    tm = u0_ref.shape[1]
    cw = y_ref.shape[2]
    my_scr[...] = _dot(y_ref[0], w_ref[cw:, :])

    scr[CONV_HALO:CONV_HALO + tm, :] = u0_ref[0].astype(F32)
    scr[0:CONV_HALO, :] = jnp.where(i > 0, halo_ref[0].astype(F32), 0.0)
    span = tm + CONV_HALO - SUBLANES
    for r in range(1, SUBLANES):
        shifted[r - 1, 0:span, :] = scr[r:r + span, :]
    off = CONV_HALO - (CONV_K - 1)
    for r0 in range(0, tm, rb):
        acc = jnp.zeros((rb // SUBLANES, SUBLANES, cw), F32) + cb_ref[...]
        for k in range(CONV_K):
            r = (off + k) % SUBLANES
            a0 = r0 + off + k - r
            src = scr[a0:a0 + rb, :] if r == 0 else shifted[r - 1, a0:a0 + rb, :]
            acc = acc + cwt_ref[k] * src.reshape(rb // SUBLANES, SUBLANES, cw)
        acc = acc.reshape(rb, cw)
        mu = jnp.mean(acc, axis=-1, keepdims=True)
        dd = acc - mu
        var = jnp.mean(dd * dd, axis=-1, keepdims=True)
        z = dd * lax.rsqrt(var + LN_EPS) * lg_ref[...] + lb_ref[...]
        u_scr[r0:r0 + rb, :] = (z * _sigmoid(z)).astype(BF16)

    m = _dot(u_scr[...], w_ref[0:cw, :]) + my_scr[...]
    x1 = x_ref[0] + mod_ref[0, 2:3, :] * _rms(m, g_ref[...])
    o_ref[0] = x1
    h_ref[0] = (_rms(x1, gf_ref[...]) * (1.0 + mod_ref[0, 4:5, :]) + mod_ref[0, 3:4, :]).astype(h_ref.dtype)


def _mixout_call(u0, conv_w, conv_b, lg, lb, y, x, mod, g, gf, w, tm, rb=16):
    b, t, d = x.shape
    cw = u0.shape[2]
    hb = tm // CONV_HALO
    row_blk = lambda width: pl.BlockSpec((1, tm, width), lambda bi, i: (bi, i, 0))
    vec = lambda width: pl.BlockSpec((1, width), lambda bi, i: (0, 0))
    return pl.pallas_call(
        functools.partial(_mixout_kernel, rb=rb),
        grid=(b, t // tm),
        in_specs=[row_blk(cw),
                  pl.BlockSpec((1, CONV_HALO, cw), lambda bi, i: (bi, jnp.maximum(i * hb - 1, 0), 0)),
                  pl.BlockSpec((CONV_K, SUBLANES, cw), lambda bi, i: (0, 0, 0)),
                  vec(cw), vec(cw), vec(cw),
                  row_blk(y.shape[2]), row_blk(d),
                  pl.BlockSpec((1, 6, d), lambda bi, i: (bi, 0, 0)),
                  vec(d), vec(d),
                  pl.BlockSpec(w.shape, lambda bi, i: (0, 0))],
        out_specs=[row_blk(d), row_blk(d)],
        out_shape=[jax.ShapeDtypeStruct((b, t, d), F32), jax.ShapeDtypeStruct((b, t, d), BF16)],
        scratch_shapes=[pltpu.VMEM((tm + CONV_HALO, cw), F32),
                        pltpu.VMEM((SUBLANES - 1, tm + CONV_HALO - SUBLANES, cw), F32),
                        pltpu.VMEM((tm, cw), BF16),
                        pltpu.VMEM((tm, d), F32)],
        compiler_params=pltpu.CompilerParams(dimension_semantics=("arbitrary", "arbitrary"),
                                             vmem_limit_bytes=VMEM_LIMIT),
        name="conv_outproj",
    )(u0, u0, conv_w, conv_b, lg, lb, y, x, mod, g, gf, w)


def _stack(val, m0, m1):
    return jnp.concatenate([val * m0, val * m1], axis=0)


def _neumann_inverse(n_w, eye_w, m0, m1):
    levels = CHUNK.bit_length() - 2
    t_w = [eye_w + n_ for n_ in n_w]
    pb = [n_.astype(BF16) for n_ in n_w]
    pk = [_dot(x, _stack(x, m0, m1)) for x in pb]
    for k in range(1, levels + 1):
        pkb = [x.astype(BF16) for x in pk]
        pd = [_stack(x, m0, m1) for x in pkb]
        if k < levels:
            both = [_dot(jnp.concatenate([t_.astype(BF16), p_], axis=0), d_)
                    for t_, p_, d_ in zip(t_w, pkb, pd)]
            t_w = [t_ + b_[:CHUNK] for t_, b_ in zip(t_w, both)]
            pk = [b_[CHUNK:] for b_ in both]
        else:
            t_w = [t_ + _dot(t_.astype(BF16), d_) for t_, d_ in zip(t_w, pd)]
    return t_w


def _rwkv_kernel(kt_ref, rt_ref, kh_ref, bh_ref, v_ref, bonus_ref, gate_ref, el_ref,
                 lg_ref, lb_ref, bd_ref, hmask_ref, strict_ref, incl_ref, eye_ref,
                 o_ref,
                 s_ref, y_ref, ak_ref, abr_ref, t_ref):
    i = pl.program_id(0)
    nb, tb, rw = o_ref.shape
    npair = rw // PAIR
    cpb = tb // CHUNK

    @pl.when(i == 0)
    def _():
        s_ref[...] = jnp.zeros_like(s_ref)

    strict = strict_ref[...]
    incl = incl_ref[...]
    eye = eye_ref[...]
    m0 = hmask_ref[0]
    m1 = hmask_ref[1]
    pairs = range(npair)
    lanes = [slice(pi * PAIR, (pi + 1) * PAIR) for pi in pairs]
    stack = lambda val: _stack(val, m0, m1)

    def rows_of(j):
        return pl.ds(pl.multiple_of(j * CHUNK, CHUNK), CHUNK)

    def phase_a(gi, carry):
        inst = [(bi, gi * A_GROUP + j, pi) for bi in range(nb) for j in range(A_GROUP) for pi in pairs]
        sl = [(bi, rows_of(j), lanes[pi]) for bi, j, pi in inst]
        lhs = [jnp.concatenate([kt_ref[x], rt_ref[x]], axis=0) for x in sl]
        kb = [jnp.concatenate([stack(kh_ref[x]), stack(bh_ref[x])], axis=0) for x in sl]
        g = [_dg(l_, k_, NT) for l_, k_ in zip(lhs, kb)]
        for g_, (bi, j, pi) in zip(g, inst):
            ci = bi * cpb + j
            ak_ref[ci, pi, 0:CHUNK] = (g_[0:CHUNK, 0:PAIR] * strict).astype(BF16)
            ak_ref[ci, pi, CHUNK:] = (g_[CHUNK:, 0:PAIR] * incl).astype(BF16)
            abr_ref[ci, pi] = (g_[CHUNK:, PAIR:] * incl).astype(BF16)
        t_w = _neumann_inverse([-(g_[0:CHUNK, PAIR:] * strict) for g_ in g], eye, m0, m1)
        for t_, (bi, j, pi) in zip(t_w, inst):
            t_ref[bi * cpb + j, pi] = t_.astype(BF16)
        return carry

    lax.fori_loop(0, cpb // A_GROUP, phase_a, 0)

    def phase_b(j, carry):
        chains = [(bi, pi) for bi in range(nb) for pi in pairs]
        cis = [bi * cpb + j for bi, _ in chains]
        rs = rows_of(j)
        sl = [(bi, rs, lanes[pi]) for bi, pi in chains]
        lhs = [jnp.concatenate([kt_ref[x], rt_ref[x]], axis=0) for x in sl]
        kb = [jnp.concatenate([stack(kh_ref[x]), -stack(bh_ref[x])], axis=0) for x in sl]
        vs = [stack(v_ref[x]) for x in sl]
        s = [s_ref[bi * npair + pi] for bi, pi in chains]
        sb = [x.astype(BF16) for x in s]
        xy = [_dg(l_, sb_, NT) + _dot(ak_ref[ci, pi], v_)
              for l_, sb_, v_, ci, (_, pi) in zip(lhs, sb, vs, cis, chains)]
        us = [stack(_dot(t_ref[ci, pi], stack(x_[0:CHUNK].astype(BF16))).astype(BF16))
              for x_, ci, (_, pi) in zip(xy, cis, chains)]
        ds_ = [_dg(jnp.concatenate([v_, u_], axis=0), k_, TN) for v_, u_, k_ in zip(vs, us, kb)]
        for s_, d_, (bi, pi) in zip(s, ds_, chains):
            e_l = el_ref[bi, j][0:1, lanes[pi]]
            s_ref[bi * npair + pi] = (s_ + d_) * e_l
        for x_, u_, x, ci, (_, pi) in zip(xy, us, sl, cis, chains):
            y_ref[x] = x_[CHUNK:] - _dot(abr_ref[ci, pi], u_)
        return carry

    lax.fori_loop(0, cpb, phase_b, 0, unroll=8)

    bd = bd_ref[...]
    y = y_ref[...].reshape(nb * tb, rw)
    inv = 1.0 / HEAD
    mean = _dot(y.astype(BF16), bd) * inv
    d = y - mean
    var = _dot((d * d).astype(BF16), bd) * inv
    yl = d * lax.rsqrt(var + LNX_EPS) * lg_ref[...] + lb_ref[...]
    bonus = bonus_ref[...].reshape(nb * tb, rw).astype(F32)
    gate = gate_ref[...].reshape(nb * tb, rw).astype(F32)
    o_ref[...] = ((yl + bonus) * gate).astype(o_ref.dtype).reshape(nb, tb, rw)


def _rwkv_call(streams, el, consts, tb):
    b, t, rw = streams[0].shape
    npair = rw // PAIR
    names = ["lnx_g", "lnx_b", "bd", "hmask", "strict", "incl", "eye"]
    ops = [consts[nm] for nm in names]
    full = lambda arr: pl.BlockSpec(arr.shape, lambda i: (0,) * arr.ndim)
    tok = pl.BlockSpec((b, tb, rw), lambda i: (0, i, 0))
    return pl.pallas_call(
        _rwkv_kernel,
        grid=(t // tb,),
        in_specs=[tok] * len(streams)
                 + [pl.BlockSpec((b, tb // CHUNK, SUBLANES, rw), lambda i: (0, i, 0, 0))]
                 + [full(a) for a in ops],
        out_specs=tok,
        out_shape=jax.ShapeDtypeStruct((b, t, rw), BF16),
        scratch_shapes=[pltpu.VMEM((b * npair, PAIR, PAIR), F32), pltpu.VMEM((b, tb, rw), F32)]
                       + [pltpu.VMEM((b * tb // CHUNK, npair, 2 * CHUNK, PAIR), BF16)]
                       + [pltpu.VMEM((b * tb // CHUNK, npair, CHUNK, PAIR), BF16)] * 2,
        compiler_params=pltpu.CompilerParams(dimension_semantics=("arbitrary",),
                                             vmem_limit_bytes=VMEM_LIMIT),
        name="rwkv_group",
    )(*streams, el, *ops)


def _ffn_kernel(x_ref, h_ref, mod_ref, gpost_ref, wup_ref, cw_ref, cb_ref, wdn_ref,
                o_ref, z_ref, zc_ref, acc_ref):
    i = pl.program_id(1)
    tm = x_ref.shape[1]
    d_ff = wdn_ref.shape[0]
    hb = h_ref[0]

    @pl.when(i == 0)
    def _():
        zc_ref[...] = jnp.zeros_like(zc_ref)

    n_chunks = d_ff // FF_CHUNK
    cols = lambda f: (slice(f * FF_CHUNK, (f + 1) * FF_CHUNK),
                      slice(d_ff + f * FF_CHUNK, d_ff + (f + 1) * FF_CHUNK))

    def up(f):
        cg, cv = cols(f)
        return jnp.concatenate([_dot(hb, wup_ref[:, cg]), _dot(hb, wup_ref[:, cv])], axis=1)

    def conv_act(f, z):
        cg, cv = cols(f)
        zb = z_ref.at[f % 2]
        zb[0:SUBLANES, :] = zc_ref[f]
        zb[SUBLANES:, :] = z
        zc_ref[f] = z[tm - SUBLANES:, :]
        zc = jnp.concatenate([cb_ref[:, cg], cb_ref[:, cv]], axis=1)
        for k in range(FFN_K):
            o = SUBLANES - (FFN_K - 1) + k
            wk = jnp.concatenate([cw_ref[k:k + 1, cg], cw_ref[k:k + 1, cv]], axis=1)
            zc = zc + wk * zb[o:o + tm, :]
        zg = zc[:, :FF_CHUNK].astype(BF16)
        return zg * _sigmoid(zg) * zc[:, FF_CHUNK:].astype(BF16)

    act = conv_act(0, up(0))
    z_next = up(1)
    for f in range(n_chunks):
        z_cur = z_next
        if f + 2 < n_chunks:
            z_next = up(f + 2)
        contrib = _dot(act, wdn_ref[f * FF_CHUNK:(f + 1) * FF_CHUNK, :])
        if f == 0:
            acc_ref[...] = contrib
        else:
            acc_ref[...] += contrib
        if f + 1 < n_chunks:
            act = conv_act(f + 1, z_cur)
    o_ref[0] = x_ref[0] + mod_ref[0, 5:6, :] * _rms(acc_ref[...], gpost_ref[...])


def _ffn_call(x1, h2, mod, gpost, wup, cw, cb, wdn, tm):
    b, t, d = x1.shape
    full = lambda arr: pl.BlockSpec(arr.shape, lambda bi, i: (0,) * arr.ndim)
    return pl.pallas_call(
        _ffn_kernel,
        grid=(b, t // tm),
        in_specs=[pl.BlockSpec((1, tm, d), lambda bi, i: (bi, i, 0)),
                  pl.BlockSpec((1, tm, d), lambda bi, i: (bi, i, 0)),
                  pl.BlockSpec((1, 6, d), lambda bi, i: (bi, 0, 0)),
                  full(gpost), full(wup), full(cw), full(cb), full(wdn)],
        out_specs=pl.BlockSpec((1, tm, d), lambda bi, i: (bi, i, 0)),
        out_shape=jax.ShapeDtypeStruct((b, t, d), F32),
        scratch_shapes=[pltpu.VMEM((2, tm + SUBLANES, 2 * FF_CHUNK), F32),
                        pltpu.VMEM((wdn.shape[0] // FF_CHUNK, SUBLANES, 2 * FF_CHUNK), F32),
                        pltpu.VMEM((tm, d), F32)],
        compiler_params=pltpu.CompilerParams(dimension_semantics=("arbitrary", "arbitrary"),
                                             vmem_limit_bytes=VMEM_LIMIT),
        name="conv_ffn",
    )(x1, h2, mod, gpost, wup, cw, cb, wdn)


def _layer(x, mod, lw, tiles):
    b, t, d = x.shape
    cw = lw["conv_dw_w"].shape[1]
    rw = lw["w0"].shape[0]
    row = lambda vec: vec.reshape(1, -1)

    mu = row(lw["rwkv_mu"])
    ch = jnp.arange(rw) // HEAD
    rr = jnp.arange(SUB)
    pp = jnp.arange(PAIR)
    tt = jnp.arange(CHUNK)
    first = (pp < HEAD).astype(BF16)
    consts = {
        "mu": mu[:, :3 * rw], "mu_lora": mu[:, 3 * rw:],
        "w0": row(lw["w0"]), "w2": lw["w2"].astype(BF16),
        "a0": row(lw["a0"]), "a2": lw["a2"].astype(BF16),
        "g2": lw["g2"].astype(BF16),
        "k_k": row(lw["k_k"]), "k_a": row(lw["k_a"]), "r_k": row(lw["r_k"]),
        "lnx_g": row(lw["lnx_g"]), "lnx_b": row(lw["lnx_b"]),
        "bd": (ch[:, None] == ch[None, :]).astype(BF16),
        "tri": ((rr[:, None] // CHUNK == rr[None, :] // CHUNK) & (rr[None, :] <= rr[:, None])).astype(BF16),
        "hmask": jnp.broadcast_to(jnp.stack([first, 1 - first])[:, None, :], (2, CHUNK, PAIR)),
        "strict": ((pp[None, :] % CHUNK) < tt[:, None]).astype(F32),
        "incl": ((pp[None, :] % CHUNK) <= tt[:, None]).astype(F32),
        "eye": ((pp[None, :] % CHUNK) == tt[:, None]).astype(F32),
    }

    u0, *streams, el = _front_call(x, mod, row(lw["mix_pre_g"]), lw["w_in"].astype(BF16), consts, 2 * cw, tiles["inproj"])
    conv_w = jnp.broadcast_to(lw["conv_dw_w"][:, None, :], (CONV_K, SUBLANES, cw))
    y = _rwkv_call(streams, el, consts, tiles["rwkv"])
    x1, h2 = _mixout_call(u0, conv_w, row(lw["conv_dw_b"]), row(lw["conv_ln_g"]), row(lw["conv_ln_b"]), y, x,
                          mod, row(lw["mix_post_g"]), row(lw["ffn_pre_g"]), lw["w_out"].astype(BF16),
                          tiles["mixout"])
    return _ffn_call(x1, h2, mod, row(lw["ffn_post_g"]), lw["w_up"].astype(BF16), lw["ffn_dw_w"],
                     row(lw["ffn_dw_b"]), lw["w_down"].astype(BF16), tiles["ffn"])


def _tiles(t):
    pick = lambda pref: min(pref, t)
    return {"inproj": pick(512), "rwkv": pick(512), "mixout": pick(512), "ffn": pick(256)}


def kernel(x, c, ada_w, ada_b, mix_pre_g, mix_post_g, w_in, conv_dw_w, conv_dw_b, conv_ln_g,
           conv_ln_b, rwkv_mu, w0, w2, a0, a2, g2, k_k, k_a, r_k, lnx_g, lnx_b, w_out, ffn_pre_g,
           ffn_post_g, w_up, ffn_dw_w, ffn_dw_b, w_down):
    b, t, d = x.shape
    depth = ada_w.shape[0]
    tiles = _tiles(t)
    c8 = jnp.pad(c, ((0, 8 - b), (0, 0))).T
    for l in range(depth):
        mod = _mod_call(c8, ada_w[l], ada_b[l].reshape(1, -1), 256, b)[:b].reshape(b, 6, d)
        lw = {"mix_pre_g": mix_pre_g[l], "mix_post_g": mix_post_g[l], "w_in": w_in[l],
              "conv_dw_w": conv_dw_w[l], "conv_dw_b": conv_dw_b[l], "conv_ln_g": conv_ln_g[l],
              "conv_ln_b": conv_ln_b[l], "rwkv_mu": rwkv_mu[l], "w0": w0[l], "w2": w2[l],
              "a0": a0[l], "a2": a2[l], "g2": g2[l], "k_k": k_k[l], "k_a": k_a[l],
              "r_k": r_k[l].reshape(-1), "lnx_g": lnx_g[l], "lnx_b": lnx_b[l], "w_out": w_out[l],
              "ffn_pre_g": ffn_pre_g[l], "ffn_post_g": ffn_post_g[l], "w_up": w_up[l],
              "ffn_dw_w": ffn_dw_w[l], "ffn_dw_b": ffn_dw_b[l], "w_down": w_down[l]}
        x = _layer(x, mod, lw, tiles)
    return x
```

```python
import functools
import math

import jax
import jax.numpy as jnp
from jax import lax
from jax.experimental import pallas as pl
from jax.experimental.pallas import tpu as pltpu

F32 = jnp.float32
BF16 = jnp.bfloat16

HEAD = 64
CHUNK = 64
PAIR = 2 * HEAD
CONV_K = 31
FFN_K = 3
RMS_EPS = 1e-6
LN_EPS = 1e-5
LNX_EPS = 64e-5
SUBLANES = 8
CONV_HALO = 32
FF_CHUNK = 256
A_GROUP = 8
SUB = 256
VMEM_LIMIT = 56 * 1024 * 1024

NT = (((1,), (1,)), ((), ()))
TN = (((0,), (0,)), ((), ()))


def _dot(a, b):
    return jnp.dot(a, b, preferred_element_type=F32)


def _dg(a, b, dims):
    return lax.dot_general(a, b, dims, preferred_element_type=F32)


def _sigmoid(x):
    return jax.nn.sigmoid(x)


def _rms(x, g):
    ms = jnp.mean(x * x, axis=-1, keepdims=True)
    return x * lax.rsqrt(ms + RMS_EPS) * g


def _mod_kernel(ct_ref, w_ref, b_ref, o_ref, *, nb):
    k = pl.program_id(0)
    cs = ct_ref[...]
    s = cs * _sigmoid(cs)
    w = w_ref[...]

    @pl.when(k == 0)
    def _():
        o_ref[...] = jnp.broadcast_to(b_ref[...], o_ref.shape)

    for bi in range(nb):
        o_ref[bi:bi + 1, :] += jnp.sum(w * s[:, bi:bi + 1], axis=0, keepdims=True)


def _mod_call(c8, ada_w, ada_b, tk, nb):
    d, n = ada_w.shape
    return pl.pallas_call(
        functools.partial(_mod_kernel, nb=nb),
        grid=(d // tk,),
        in_specs=[pl.BlockSpec((tk, 8), lambda k: (k, 0)),
                  pl.BlockSpec((tk, n), lambda k: (k, 0)),
                  pl.BlockSpec((1, n), lambda k: (0, 0))],
        out_specs=pl.BlockSpec((8, n), lambda k: (0, 0)),
        out_shape=jax.ShapeDtypeStruct((8, n), F32),
        compiler_params=pltpu.CompilerParams(dimension_semantics=("arbitrary",),
                                             vmem_limit_bytes=VMEM_LIMIT),
        name="adaln_mod",
    )(c8, ada_w, ada_b)


def _front_kernel(x_ref, mod_ref, g_ref, w_ref, mu_ref, mul_ref, w0_ref, w2_ref, a0_ref, a2_ref, g2_ref,
                  kk_ref, ka_ref, rk_ref, bd_ref, tri_ref,
                  u0_ref, kt_ref, rt_ref, kh_ref, bh_ref, v_ref, bonus_ref, gate_ref, el_ref,
                  prev_ref, prevl_ref):
    i = pl.program_id(1)
    tm = x_ref.shape[1]
    cw = u0_ref.shape[2]
    nc = 2 * cw
    rw = v_ref.shape[2]

    @pl.when(i == 0)
    def _():
        prev_ref[...] = jnp.zeros_like(prev_ref)
        prevl_ref[...] = jnp.zeros_like(prevl_ref)

    bd = bd_ref[...]
    tri = tri_ref[...]
    rows = lax.broadcasted_iota(jnp.int32, (SUB, 1), 0)

    def shifted(p, prev_row, mu):
        prev = jnp.where(rows == 0, prev_row[0:1, :], pltpu.roll(p, 1, axis=0))
        prev_row[0:1, :] = p[SUB - 1:SUB, :]
        return p + (prev - p) * mu

    def prep(p, pl_, r0):
        xs = shifted(p, prev_ref, mu_ref[...])
        xl = shifted(pl_, prevl_ref, mul_ref[...])
        r = xs[:, 0:rw]
        k = xs[:, rw:2 * rw]
        v = xs[:, 2 * rw:3 * rw]
        dl = w2_ref.shape[0]
        al = a2_ref.shape[0]
        wd = xl[:, 0:dl]
        ad = xl[:, dl:dl + al]
        gd = xl[:, dl + al:]
        zw = w0_ref[...] + _dot(jnp.tanh(wd).astype(BF16), w2_ref[...])
        lw = -math.exp(-0.5) * _sigmoid(zw)
        a = _sigmoid(a0_ref[...] + _dot(ad.astype(BF16), a2_ref[...]))
        gate = _dot(_sigmoid(gd).astype(BF16), g2_ref[...])
        kkr = k * kk_ref[...]
        ss = _dot((kkr * kkr).astype(BF16), bd)
        kk = kkr * lax.rsqrt(jnp.maximum(ss, 1e-24))
        km = k * (1.0 + (a - 1.0) * ka_ref[...])
        bonus = _dot((r * km * rk_ref[...]).astype(BF16), bd) * v
        hi = lw.astype(BF16)
        lo = (lw - hi.astype(F32)).astype(BF16)
        c = _dot(tri, hi) + _dot(tri, lo)
        ec = jnp.exp(c)
        ecn = jnp.exp(-c)
        rs = slice(r0, r0 + SUB)
        kt_ref[0, rs, :] = (kk * jnp.exp(c - lw)).astype(BF16)
        rt_ref[0, rs, :] = (r * ec).astype(BF16)
        kh_ref[0, rs, :] = (km * ecn).astype(BF16)
        bh_ref[0, rs, :] = (a * kk * ecn).astype(BF16)
        v_ref[0, rs, :] = v.astype(BF16)
        bonus_ref[0, rs, :] = bonus.astype(BF16)
        gate_ref[0, rs, :] = gate.astype(BF16)
        for j in range(SUB // CHUNK):
            last = ec[(j + 1) * CHUNK - 1:(j + 1) * CHUNK, :]
            el_ref[0, r0 // CHUNK + j] = jnp.broadcast_to(last, (SUBLANES, rw))

    n_main = nc + 3 * rw

    def proj(r0):
        x = x_ref[0, r0:r0 + SUB, :]
        h = (_rms(x, g_ref[...]) * (1.0 + mod_ref[0, 1:2, :]) + mod_ref[0, 0:1, :]).astype(BF16)
        return _dot(h, w_ref[:, 0:n_main]), _dot(h, w_ref[:, n_main:])

    p_next = proj(0)
    for r0 in range(0, tm, SUB):
        p = p_next
        if r0 + SUB < tm:
            p_next = proj(r0 + SUB)
        u0_ref[0, r0:r0 + SUB, :] = (p[0][:, :cw] * _sigmoid(p[0][:, cw:nc])).astype(BF16)
        prep(p[0][:, nc:], p[1], r0)


def _front_call(x, mod, g, w, consts, n_conv, tm):
    b, t, d = x.shape
    n = w.shape[1]
    rw = consts["w0"].shape[1]
    names = ["mu", "mu_lora", "w0", "w2", "a0", "a2", "g2", "k_k", "k_a", "r_k", "bd", "tri"]
    ops = [consts[nm] for nm in names]
    full = lambda arr: pl.BlockSpec(arr.shape, lambda bi, i: (0,) * arr.ndim)
    tok = lambda width: pl.BlockSpec((1, tm, width), lambda bi, i: (bi, i, 0))
    tok_shape = lambda width, dt: jax.ShapeDtypeStruct((b, t, width), dt)
    return pl.pallas_call(
        _front_kernel,
        grid=(b, t // tm),
        in_specs=[tok(d), pl.BlockSpec((1, 6, d), lambda bi, i: (bi, 0, 0)),
                  pl.BlockSpec((1, d), lambda bi, i: (0, 0)),
                  pl.BlockSpec((d, n), lambda bi, i: (0, 0))] + [full(a) for a in ops],
        out_specs=[tok(n_conv // 2)] + [tok(rw)] * 7
                  + [pl.BlockSpec((1, tm // CHUNK, SUBLANES, rw), lambda bi, i: (bi, i, 0, 0))],
        out_shape=[tok_shape(n_conv // 2, BF16)] + [tok_shape(rw, BF16)] * 7
                  + [jax.ShapeDtypeStruct((b, t // CHUNK, SUBLANES, rw), F32)],
        scratch_shapes=[pltpu.VMEM((SUBLANES, 3 * rw), F32),
                        pltpu.VMEM((SUBLANES, n - n_conv - 3 * rw), F32)],
        compiler_params=pltpu.CompilerParams(dimension_semantics=("arbitrary", "arbitrary"),
                                             vmem_limit_bytes=VMEM_LIMIT),
        name="inproj_prep",
    )(x, mod, g, w, *ops)


def _mixout_kernel(u0_ref, halo_ref, cwt_ref, cb_ref, lg_ref, lb_ref, y_ref, x_ref, mod_ref, g_ref, gf_ref,
                   w_ref, o_ref, h_ref, scr, shifted, u_scr, my_scr, *, rb):
    i = pl.program_id(1)
    tm = u0_ref.shape[1]
    cw = y_ref.shape[2]
    my_scr[...] = _dot(y_ref[0], w_ref[cw:, :])

    scr[CONV_HALO:CONV_HALO + tm, :] = u0_ref[0].astype(F32)
    scr[0:CONV_HALO, :] = jnp.where(i > 0, halo_ref[0].astype(F32), 0.0)
    span = tm + CONV_HALO - SUBLANES
    for r in range(1, SUBLANES):
        shifted[r - 1, 0:span, :] = scr[r:r + span, :]
    off = CONV_HALO - (CONV_K - 1)
    for r0 in range(0, tm, rb):
        acc = jnp.zeros((rb // SUBLANES, SUBLANES, cw), F32) + cb_ref[...]
        for k in range(CONV_K):
            r = (off + k) % SUBLANES
            a0 = r0 + off + k - r
            src = scr[a0:a0 + rb, :] if r == 0 else shifted[r - 1, a0:a0 + rb, :]
            acc = acc + cwt_ref[k] * src.reshape(rb // SUBLANES, SUBLANES, cw)
        acc = acc.reshape(rb, cw)
        mu = jnp.mean(acc, axis=-1, keepdims=True)
        dd = acc - mu
        var = jnp.mean(dd * dd, axis=-1, keepdims=True)
        z = dd * lax.rsqrt(var + LN_EPS) * lg_ref[...] + lb_ref[...]
        u_scr[r0:r0 + rb, :] = (z * _sigmoid(z)).astype(BF16)

    m = _dot(u_scr[...], w_ref[0:cw, :]) + my_scr[...]
    x1 = x_ref[0] + mod_ref[0, 2:3, :] * _rms(m, g_ref[...])
    o_ref[0] = x1
    h_ref[0] = (_rms(x1, gf_ref[...]) * (1.0 + mod_ref[0, 4:5, :]) + mod_ref[0, 3:4, :]).astype(h_ref.dtype)


def _mixout_call(u0, conv_w, conv_b, lg, lb, y, x, mod, g, gf, w, tm, rb=16):
    b, t, d = x.shape
    cw = u0.shape[2]
    hb = tm // CONV_HALO
    row_blk = lambda width: pl.BlockSpec((1, tm, width), lambda bi, i: (bi, i, 0))
    vec = lambda width: pl.BlockSpec((1, width), lambda bi, i: (0, 0))
    return pl.pallas_call(
        functools.partial(_mixout_kernel, rb=rb),
        grid=(b, t // tm),
        in_specs=[row_blk(cw),
                  pl.BlockSpec((1, CONV_HALO, cw), lambda bi, i: (bi, jnp.maximum(i * hb - 1, 0), 0)),
                  pl.BlockSpec((CONV_K, SUBLANES, cw), lambda bi, i: (0, 0, 0)),
                  vec(cw), vec(cw), vec(cw),
                  row_blk(y.shape[2]), row_blk(d),
                  pl.BlockSpec((1, 6, d), lambda bi, i: (bi, 0, 0)),
                  vec(d), vec(d),
                  pl.BlockSpec(w.shape, lambda bi, i: (0, 0))],
        out_specs=[row_blk(d), row_blk(d)],
        out_shape=[jax.ShapeDtypeStruct((b, t, d), F32), jax.ShapeDtypeStruct((b, t, d), BF16)],
        scratch_shapes=[pltpu.VMEM((tm + CONV_HALO, cw), F32),
                        pltpu.VMEM((SUBLANES - 1, tm + CONV_HALO - SUBLANES, cw), F32),
                        pltpu.VMEM((tm, cw), BF16),
                        pltpu.VMEM((tm, d), F32)],
        compiler_params=pltpu.CompilerParams(dimension_semantics=("arbitrary", "arbitrary"),
                                             vmem_limit_bytes=VMEM_LIMIT),
        name="conv_outproj",
    )(u0, u0, conv_w, conv_b, lg, lb, y, x, mod, g, gf, w)


def _stack(val, m0, m1):
    return jnp.concatenate([val * m0, val * m1], axis=0)


def _neumann_inverse(n_w, eye_w, m0, m1):
    levels = CHUNK.bit_length() - 2
    t_w = [eye_w + n_ for n_ in n_w]
    pb = [n_.astype(BF16) for n_ in n_w]
    pk = [_dot(x, _stack(x, m0, m1)) for x in pb]
    for k in range(1, levels + 1):
        pkb = [x.astype(BF16) for x in pk]
        pd = [_stack(x, m0, m1) for x in pkb]
        if k < levels:
            both = [_dot(jnp.concatenate([t_.astype(BF16), p_], axis=0), d_)
                    for t_, p_, d_ in zip(t_w, pkb, pd)]
            t_w = [t_ + b_[:CHUNK] for t_, b_ in zip(t_w, both)]
            pk = [b_[CHUNK:] for b_ in both]
        else:
            t_w = [t_ + _dot(t_.astype(BF16), d_) for t_, d_ in zip(t_w, pd)]
    return t_w


def _rwkv_kernel(kt_ref, rt_ref, kh_ref, bh_ref, v_ref, bonus_ref, gate_ref, el_ref,
                 lg_ref, lb_ref, bd_ref, hmask_ref, strict_ref, incl_ref, eye_ref,
                 o_ref,
                 s_ref, y_ref, ak_ref, abr_ref, t_ref):
    i = pl.program_id(0)
    nb, tb, rw = o_ref.shape
    npair = rw // PAIR
    cpb = tb // CHUNK

    @pl.when(i == 0)
    def _():
        s_ref[...] = jnp.zeros_like(s_ref)

    strict = strict_ref[...]
    incl = incl_ref[...]
    eye = eye_ref[...]
    m0 = hmask_ref[0]
    m1 = hmask_ref[1]
    pairs = range(npair)
    lanes = [slice(pi * PAIR, (pi + 1) * PAIR) for pi in pairs]
    stack = lambda val: _stack(val, m0, m1)

    def rows_of(j):
        return pl.ds(pl.multiple_of(j * CHUNK, CHUNK), CHUNK)

    def phase_a(gi, carry):
        inst = [(bi, gi * A_GROUP + j, pi) for bi in range(nb) for j in range(A_GROUP) for pi in pairs]
        sl = [(bi, rows_of(j), lanes[pi]) for bi, j, pi in inst]
        lhs = [jnp.concatenate([kt_ref[x], rt_ref[x]], axis=0) for x in sl]
        kb = [jnp.concatenate([stack(kh_ref[x]), stack(bh_ref[x])], axis=0) for x in sl]
        g = [_dg(l_, k_, NT) for l_, k_ in zip(lhs, kb)]
        for g_, (bi, j, pi) in zip(g, inst):
            ci = bi * cpb + j
            ak_ref[ci, pi, 0:CHUNK] = (g_[0:CHUNK, 0:PAIR] * strict).astype(BF16)
            ak_ref[ci, pi, CHUNK:] = (g_[CHUNK:, 0:PAIR] * incl).astype(BF16)
            abr_ref[ci, pi] = (g_[CHUNK:, PAIR:] * incl).astype(BF16)
        t_w = _neumann_inverse([-(g_[0:CHUNK, PAIR:] * strict) for g_ in g], eye, m0, m1)
        for t_, (bi, j, pi) in zip(t_w, inst):
            t_ref[bi * cpb + j, pi] = t_.astype(BF16)
        return carry

    lax.fori_loop(0, cpb // A_GROUP, phase_a, 0)

    def phase_b(j, carry):
        chains = [(bi, pi) for bi in range(nb) for pi in pairs]
        cis = [bi * cpb + j for bi, _ in chains]
        rs = rows_of(j)
        sl = [(bi, rs, lanes[pi]) for bi, pi in chains]
        lhs = [jnp.concatenate([kt_ref[x], rt_ref[x]], axis=0) for x in sl]
        kb = [jnp.concatenate([stack(kh_ref[x]), -stack(bh_ref[x])], axis=0) for x in sl]
        vs = [stack(v_ref[x]) for x in sl]
        s = [s_ref[bi * npair + pi] for bi, pi in chains]
        sb = [x.astype(BF16) for x in s]
        xy = [_dg(l_, sb_, NT) + _dot(ak_ref[ci, pi], v_)
              for l_, sb_, v_, ci, (_, pi) in zip(lhs, sb, vs, cis, chains)]
        us = [stack(_dot(t_ref[ci, pi], stack(x_[0:CHUNK].astype(BF16))).astype(BF16))
              for x_, ci, (_, pi) in zip(xy, cis, chains)]
        ds_ = [_dg(jnp.concatenate([v_, u_], axis=0), k_, TN) for v_, u_, k_ in zip(vs, us, kb)]
        for s_, d_, (bi, pi) in zip(s, ds_, chains):
            e_l = el_ref[bi, j][0:1, lanes[pi]]
            s_ref[bi * npair + pi] = (s_ + d_) * e_l
        for x_, u_, x, ci, (_, pi) in zip(xy, us, sl, cis, chains):
            y_ref[x] = x_[CHUNK:] - _dot(abr_ref[ci, pi], u_)
        return carry

    lax.fori_loop(0, cpb, phase_b, 0, unroll=8)

    bd = bd_ref[...]
    y = y_ref[...].reshape(nb * tb, rw)
    inv = 1.0 / HEAD
    mean = _dot(y.astype(BF16), bd) * inv
    d = y - mean
    var = _dot((d * d).astype(BF16), bd) * inv
    yl = d * lax.rsqrt(var + LNX_EPS) * lg_ref[...] + lb_ref[...]
    bonus = bonus_ref[...].reshape(nb * tb, rw).astype(F32)
    gate = gate_ref[...].reshape(nb * tb, rw).astype(F32)
    o_ref[...] = ((yl + bonus) * gate).astype(o_ref.dtype).reshape(nb, tb, rw)


def _rwkv_call(streams, el, consts, tb):
    b, t, rw = streams[0].shape
    npair = rw // PAIR
    names = ["lnx_g", "lnx_b", "bd", "hmask", "strict", "incl", "eye"]
    ops = [consts[nm] for nm in names]
    full = lambda arr: pl.BlockSpec(arr.shape, lambda i: (0,) * arr.ndim)
    tok = pl.BlockSpec((b, tb, rw), lambda i: (0, i, 0))
    return pl.pallas_call(
        _rwkv_kernel,
        grid=(t // tb,),
        in_specs=[tok] * len(streams)
                 + [pl.BlockSpec((b, tb // CHUNK, SUBLANES, rw), lambda i: (0, i, 0, 0))]
                 + [full(a) for a in ops],
        out_specs=tok,
        out_shape=jax.ShapeDtypeStruct((b, t, rw), BF16),
        scratch_shapes=[pltpu.VMEM((b * npair, PAIR, PAIR), F32), pltpu.VMEM((b, tb, rw), F32)]
                       + [pltpu.VMEM((b * tb // CHUNK, npair, 2 * CHUNK, PAIR), BF16)]
                       + [pltpu.VMEM((b * tb // CHUNK, npair, CHUNK, PAIR), BF16)] * 2,
        compiler_params=pltpu.CompilerParams(dimension_semantics=("arbitrary",),
                                             vmem_limit_bytes=VMEM_LIMIT),
        name="rwkv_group",
    )(*streams, el, *ops)


def _ffn_kernel(x_ref, h_ref, mod_ref, gpost_ref, wup_ref, cw_ref, cb_ref, wdn_ref,
                o_ref, z_ref, zc_ref, acc_ref):
    i = pl.program_id(1)
    tm = x_ref.shape[1]
    d_ff = wdn_ref.shape[0]
    hb = h_ref[0]

    @pl.when(i == 0)
    def _():
        zc_ref[...] = jnp.zeros_like(zc_ref)

    n_chunks = d_ff // FF_CHUNK
    cols = lambda f: (slice(f * FF_CHUNK, (f + 1) * FF_CHUNK),
                      slice(d_ff + f * FF_CHUNK, d_ff + (f + 1) * FF_CHUNK))

    def up(f):
        cg, cv = cols(f)
        return jnp.concatenate([_dot(hb, wup_ref[:, cg]), _dot(hb, wup_ref[:, cv])], axis=1)

    def conv_act(f, z):
        cg, cv = cols(f)
        zb = z_ref.at[f % 2]
        zb[0:SUBLANES, :] = zc_ref[f]
        zb[SUBLANES:, :] = z
        zc_ref[f] = z[tm - SUBLANES:, :]
        zc = jnp.concatenate([cb_ref[:, cg], cb_ref[:, cv]], axis=1)
        for k in range(FFN_K):
            o = SUBLANES - (FFN_K - 1) + k
            wk = jnp.concatenate([cw_ref[k:k + 1, cg], cw_ref[k:k + 1, cv]], axis=1)
            zc = zc + wk * zb[o:o + tm, :]
        zg = zc[:, :FF_CHUNK].astype(BF16)
        return zg * _sigmoid(zg) * zc[:, FF_CHUNK:].astype(BF16)

    act = conv_act(0, up(0))
    z_next = up(1)
    for f in range(n_chunks):
        z_cur = z_next
        if f + 2 < n_chunks:
            z_next = up(f + 2)
        contrib = _dot(act, wdn_ref[f * FF_CHUNK:(f + 1) * FF_CHUNK, :])
        if f == 0:
            acc_ref[...] = contrib
        else:
            acc_ref[...] += contrib
        if f + 1 < n_chunks:
            act = conv_act(f + 1, z_cur)
    o_ref[0] = x_ref[0] + mod_ref[0, 5:6, :] * _rms(acc_ref[...], gpost_ref[...])


def _ffn_call(x1, h2, mod, gpost, wup, cw, cb, wdn, tm):
    b, t, d = x1.shape
    full = lambda arr: pl.BlockSpec(arr.shape, lambda bi, i: (0,) * arr.ndim)
    return pl.pallas_call(
        _ffn_kernel,
        grid=(b, t // tm),
        in_specs=[pl.BlockSpec((1, tm, d), lambda bi, i: (bi, i, 0)),
                  pl.BlockSpec((1, tm, d), lambda bi, i: (bi, i, 0)),
                  pl.BlockSpec((1, 6, d), lambda bi, i: (bi, 0, 0)),
                  full(gpost), full(wup), full(cw), full(cb), full(wdn)],
        out_specs=pl.BlockSpec((1, tm, d), lambda bi, i: (bi, i, 0)),
        out_shape=jax.ShapeDtypeStruct((b, t, d), F32),
        scratch_shapes=[pltpu.VMEM((2, tm + SUBLANES, 2 * FF_CHUNK), F32),
                        pltpu.VMEM((wdn.shape[0] // FF_CHUNK, SUBLANES, 2 * FF_CHUNK), F32),
                        pltpu.VMEM((tm, d), F32)],
        compiler_params=pltpu.CompilerParams(dimension_semantics=("arbitrary", "arbitrary"),
                                             vmem_limit_bytes=VMEM_LIMIT),
        name="conv_ffn",
    )(x1, h2, mod, gpost, wup, cw, cb, wdn)


def _layer(x, mod, lw, tiles):
    b, t, d = x.shape
    cw = lw["conv_dw_w"].shape[1]
    rw = lw["w0"].shape[0]
    row = lambda vec: vec.reshape(1, -1)

    mu = row(lw["rwkv_mu"])
    ch = jnp.arange(rw) // HEAD
    rr = jnp.arange(SUB)
    pp = jnp.arange(PAIR)
    tt = jnp.arange(CHUNK)
    first = (pp < HEAD).astype(BF16)
    consts = {
        "mu": mu[:, :3 * rw], "mu_lora": mu[:, 3 * rw:],
        "w0": row(lw["w0"]), "w2": lw["w2"].astype(BF16),
        "a0": row(lw["a0"]), "a2": lw["a2"].astype(BF16),
        "g2": lw["g2"].astype(BF16),
        "k_k": row(lw["k_k"]), "k_a": row(lw["k_a"]), "r_k": row(lw["r_k"]),
        "lnx_g": row(lw["lnx_g"]), "lnx_b": row(lw["lnx_b"]),
        "bd": (ch[:, None] == ch[None, :]).astype(BF16),
        "tri": ((rr[:, None] // CHUNK == rr[None, :] // CHUNK) & (rr[None, :] <= rr[:, None])).astype(BF16),
        "hmask": jnp.broadcast_to(jnp.stack([first, 1 - first])[:, None, :], (2, CHUNK, PAIR)),
        "strict": ((pp[None, :] % CHUNK) < tt[:, None]).astype(F32),
        "incl": ((pp[None, :] % CHUNK) <= tt[:, None]).astype(F32),
        "eye": ((pp[None, :] % CHUNK) == tt[:, None]).astype(F32),
    }

    u0, *streams, el = _front_call(x, mod, row(lw["mix_pre_g"]), lw["w_in"].astype(BF16), consts, 2 * cw, tiles["inproj"])
    conv_w = jnp.broadcast_to(lw["conv_dw_w"][:, None, :], (CONV_K, SUBLANES, cw))
    y = _rwkv_call(streams, el, consts, tiles["rwkv"])
    x1, h2 = _mixout_call(u0, conv_w, row(lw["conv_dw_b"]), row(lw["conv_ln_g"]), row(lw["conv_ln_b"]), y, x,
                          mod, row(lw["mix_post_g"]), row(lw["ffn_pre_g"]), lw["w_out"].astype(BF16),
                          tiles["mixout"])
    return _ffn_call(x1, h2, mod, row(lw["ffn_post_g"]), lw["w_up"].astype(BF16), lw["ffn_dw_w"],
                     row(lw["ffn_dw_b"]), lw["w_down"].astype(BF16), tiles["ffn"])


def _tiles(t):
    pick = lambda pref: min(pref, t)
    return {"inproj": pick(1024), "rwkv": pick(512), "mixout": pick(512), "ffn": pick(256)}


def kernel(x, c, ada_w, ada_b, mix_pre_g, mix_post_g, w_in, conv_dw_w, conv_dw_b, conv_ln_g,
           conv_ln_b, rwkv_mu, w0, w2, a0, a2, g2, k_k, k_a, r_k, lnx_g, lnx_b, w_out, ffn_pre_g,
           ffn_post_g, w_up, ffn_dw_w, ffn_dw_b, w_down):
    b, t, d = x.shape
    depth = ada_w.shape[0]
    tiles = _tiles(t)
    c8 = jnp.pad(c, ((0, 8 - b), (0, 0))).T
    for l in range(depth):
        mod = _mod_call(c8, ada_w[l], ada_b[l].reshape(1, -1), 256, b)[:b].reshape(b, 6, d)
        lw = {"mix_pre_g": mix_pre_g[l], "mix_post_g": mix_post_g[l], "w_in": w_in[l],
              "conv_dw_w": conv_dw_w[l], "conv_dw_b": conv_dw_b[l], "conv_ln_g": conv_ln_g[l],
              "conv_ln_b": conv_ln_b[l], "rwkv_mu": rwkv_mu[l], "w0": w0[l], "w2": w2[l],
              "a0": a0[l], "a2": a2[l], "g2": g2[l], "k_k": k_k[l], "k_a": k_a[l],
              "r_k": r_k[l].reshape(-1), "lnx_g": lnx_g[l], "lnx_b": lnx_b[l], "w_out": w_out[l],
              "ffn_pre_g": ffn_pre_g[l], "ffn_post_g": ffn_post_g[l], "w_up": w_up[l],
              "ffn_dw_w": ffn_dw_w[l], "ffn_dw_b": ffn_dw_b[l], "w_down": w_down[l]}
        x = _layer(x, mod, lw, tiles)
    return x
```
